```python
import jax, jax.numpy as jnp
from jax import lax
import numpy as np

D_MODEL = 1024
BATCH = 16
SEQ = 2048
DEPTH = 1
DEC_BATCH = 32
DEC_SEQ = 2048
PAST_LEN = 128

GRID_W = 64
MIX_W = D_MODEL
NA_HEADS = 8
NA_HEAD_DIM = 64
NA_WIN_ROWS = 8
NA_WIN_COLS = 16
NA_QBLOCK_COLS = 16
GDN_HEADS = 4
GDN_DK = 128
GDN_DV = 128
GDN_CONV = 5
GDN_CHUNK = 64
D_FF = 2816
FFN_CONV = 3
LN_EPS = 1e-5
RMS_EPS = 1e-6
DEEPNORM_ALPHA = (2 * DEPTH) ** 0.25
DEEPNORM_BETA = (8 * DEPTH) ** -0.25

NA_WIDTH = NA_HEADS * NA_HEAD_DIM
GDN_QK_WIDTH = GDN_HEADS * GDN_DK
GDN_V_WIDTH = GDN_HEADS * GDN_DV
GDN_CONV_CH = 2 * GDN_QK_WIDTH + GDN_V_WIDTH
IN_SPLITS = (NA_WIDTH, NA_WIDTH, NA_WIDTH, GDN_QK_WIDTH, GDN_QK_WIDTH, GDN_V_WIDTH, GDN_V_WIDTH,
             GDN_HEADS, GDN_HEADS, GDN_HEADS, GDN_HEADS)
IN_COLS = sum(IN_SPLITS)

kernel_name = 'hymba_natten_gdn_encoder'


def _layer_norm(x, g, b):
    xf = x.astype(jnp.float32)
    mu = jnp.mean(xf, -1, keepdims=True)
    var = jnp.mean(jnp.square(xf - mu), -1, keepdims=True)
    y = (xf - mu) * lax.rsqrt(var + LN_EPS) * g.astype(jnp.float32) + b.astype(jnp.float32)
    return y.astype(x.dtype)


def _l2norm(t):
    return t * lax.rsqrt(jnp.sum(t * t, -1, keepdims=True) + RMS_EPS)


def _dwconv_centred(x, w):
    k = w.shape[0]
    pad = k // 2
    s = x.shape[1]
    xp = jnp.pad(x, ((0, 0), (pad, pad), (0, 0)))
    y = xp[:, 0:s] * w[0]
    for j in range(1, k):
        y = y + xp[:, j:j + s] * w[j]
    return y


def _na_col_tables():
    ncb = GRID_W // NA_QBLOCK_COLS
    span = NA_QBLOCK_COLS + NA_WIN_COLS
    kc0 = np.clip(np.arange(ncb) * NA_QBLOCK_COLS - NA_WIN_COLS // 2, 0, GRID_W - span)
    kcols = kc0[:, None] + np.arange(span)
    qcols = np.arange(GRID_W).reshape(ncb, NA_QBLOCK_COLS)
    start = np.clip(qcols - NA_WIN_COLS // 2, 0, GRID_W - NA_WIN_COLS)
    kc = kcols[:, None, :]
    colmask = (kc >= start[..., None]) & (kc < start[..., None] + NA_WIN_COLS)
    dc_idx = np.clip(kc - qcols[..., None] + NA_WIN_COLS - 1, 0, 2 * NA_WIN_COLS - 2)
    return kcols, colmask, dc_idx


def _neighbourhood_attention(q, k, v, rpb):
    b, h, rows, w, hd = q.shape
    wr = min(NA_WIN_ROWS, rows)
    kcols, colmask, dc_idx = _na_col_tables()
    ncb, span = kcols.shape
    bias_c = rpb[:, :, dc_idx].astype(jnp.float32)
    colmask_b = jnp.asarray(colmask)[:, :, None, :]

    def row(r):
        rs = jnp.clip(r - wr // 2, 0, rows - wr)
        qr = lax.dynamic_index_in_dim(q, r, axis=2, keepdims=False).reshape(b, h, ncb, NA_QBLOCK_COLS, hd)
        kr = lax.dynamic_slice_in_dim(k, rs, wr, axis=2)[:, :, :, kcols]
        vr = lax.dynamic_slice_in_dim(v, rs, wr, axis=2)[:, :, :, kcols]
        s = jnp.einsum('bhnqd,bhrnkd->bhnqrk', qr, kr).astype(jnp.float32)
        dr_idx = rs + jnp.arange(wr) - r + (NA_WIN_ROWS - 1)
        bias = jnp.take(bias_c, dr_idx, axis=1).transpose(0, 2, 3, 1, 4)
        s = jnp.where(colmask_b, s + bias, -1e30)
        p = jax.nn.softmax(s.reshape(b, h, ncb, NA_QBLOCK_COLS, wr * span), axis=-1)
        p = p.reshape(s.shape).astype(v.dtype)
        o = jnp.einsum('bhnqrk,bhrnkd->bhnqd', p, vr)
        return o.reshape(b, h, w, hd)

    out = lax.map(row, jnp.arange(rows))
    return out.transpose(1, 0, 3, 2, 4).reshape(b, rows * w, h * hd)


def _gated_delta_chunked(q, k, v, beta, g):
    b, s, h, dk = q.shape
    dv = v.shape[-1]
    c = GDN_CHUNK
    n = s // c
    chunks = lambda t: t.reshape(b, n, c, h, -1).transpose(0, 3, 1, 2, 4)
    q, k, v = chunks(q), chunks(k), chunks(v)
    beta = beta.reshape(b, n, c, h).transpose(0, 3, 1, 2)
    g = g.reshape(b, n, c, h).transpose(0, 3, 1, 2)
    gc = jnp.cumsum(g, axis=-1)
    tril = jnp.asarray(np.tril(np.ones((c, c), dtype=bool)))
    strict = jnp.asarray(np.tril(np.ones((c, c), dtype=bool), -1))
    decay = jnp.exp(jnp.where(tril, gc[..., :, None] - gc[..., None, :], -jnp.inf))
    kb = k * beta[..., None]
    lower = jnp.where(strict, jnp.einsum('bhnid,bhnjd->bhnij', kb, k) * decay, 0.0)
    a = lower + jnp.eye(c, dtype=jnp.float32)
    rhs = jnp.concatenate([v * beta[..., None], kb * jnp.exp(gc)[..., None]], axis=-1)
    sol = lax.linalg.triangular_solve(a, rhs, left_side=True, lower=True, unit_diagonal=True)
    u, wk = sol[..., :dv], sol[..., dv:]
    attn = jnp.einsum('bhnid,bhnjd->bhnij', q, k) * decay
    q_dec = q * jnp.exp(gc)[..., None]
    k_dec = k * jnp.exp(gc[..., -1:] - gc)[..., None]
    chunk_decay = jnp.exp(gc[..., -1])
    xs = tuple(jnp.moveaxis(t, 2, 0) for t in (u, wk, attn, q_dec, k_dec, chunk_decay))

    def step(state, inp):
        u_i, w_i, attn_i, qd_i, kd_i, cd_i = inp
        v_new = u_i - jnp.einsum('bhcd,bhde->bhce', w_i, state)
        o = jnp.einsum('bhcd,bhde->bhce', qd_i, state) + jnp.einsum('bhij,bhje->bhie', attn_i, v_new)
        state = state * cd_i[..., None, None] + jnp.einsum('bhcd,bhce->bhde', kd_i, v_new)
        return state, o

    state0 = jnp.zeros((b, h, dk, dv), jnp.float32)
    _, o = lax.scan(step, state0, xs)
    return o.transpose(1, 0, 3, 2, 4).reshape(b, s, h, dv)


def _layer(x, w_in, na_rpb, gdn_conv_w, gdn_a_log, gdn_dt_bias, gdn_norm_w, w_out,
           ln1_g, ln1_b, ffn_w_up, ffn_conv_w, ffn_conv_b, ffn_w_down, ln2_g, ln2_b):
    b, s, _ = x.shape
    rows = s // GRID_W
    f32 = jnp.float32
    proj = x @ w_in
    na_q, na_k, na_v, g_q, g_k, g_v, g_z, bf, bb, af, ab = jnp.split(
        proj, np.cumsum(IN_SPLITS)[:-1].tolist(), axis=-1)
    grid = lambda t: t.reshape(b, rows, GRID_W, NA_HEADS, NA_HEAD_DIM).transpose(0, 3, 1, 2, 4)
    na_out = _neighbourhood_attention(grid(na_q * NA_HEAD_DIM ** -0.5), grid(na_k), grid(na_v), na_rpb)
    qkv = jax.nn.silu(_dwconv_centred(jnp.concatenate([g_q, g_k, g_v], -1), gdn_conv_w)).astype(f32)
    gq, gk, gv = jnp.split(qkv, [GDN_QK_WIDTH, 2 * GDN_QK_WIDTH], axis=-1)
    heads = lambda t: t.reshape(b, s, GDN_HEADS, -1)
    gq = _l2norm(heads(gq)) * GDN_DK ** -0.5
    gk = _l2norm(heads(gk))
    gv = heads(gv)
    a_log = gdn_a_log.astype(f32)
    dt_bias = gdn_dt_bias.astype(f32)
    beta_f = jax.nn.sigmoid(bf.astype(f32))
    beta_b = jax.nn.sigmoid(bb.astype(f32))
    g_f = -jnp.exp(a_log[0]) * jax.nn.softplus(af.astype(f32) + dt_bias[0])
    g_b = -jnp.exp(a_log[1]) * jax.nn.softplus(ab.astype(f32) + dt_bias[1])
    flip = lambda t: jnp.flip(t, axis=1)
    o_f = _gated_delta_chunked(gq, gk, gv, beta_f, g_f)
    o_b = flip(_gated_delta_chunked(flip(gq), flip(gk), flip(gv), flip(beta_b), flip(g_b)))
    o = o_f + o_b
    o = o * lax.rsqrt(jnp.mean(o * o, -1, keepdims=True) + RMS_EPS) * gdn_norm_w.astype(f32) \
        * jax.nn.silu(heads(g_z).astype(f32))
    gdn_out = o.reshape(b, s, GDN_V_WIDTH).astype(x.dtype)
    mix = jnp.concatenate([na_out, gdn_out], axis=-1) @ w_out
    x = _layer_norm(DEEPNORM_ALPHA * x + mix, ln1_g, ln1_b)
    hdn = _dwconv_centred(x @ ffn_w_up, ffn_conv_w) + ffn_conv_b
    gate, val = jnp.split(hdn, 2, axis=-1)
    ffn = (jax.nn.silu(gate) * val) @ ffn_w_down
    return _layer_norm(DEEPNORM_ALPHA * x + ffn, ln2_g, ln2_b)


def _trunk(x, w_in, na_rpb, gdn_conv_w, gdn_a_log, gdn_dt_bias, gdn_norm_w, w_out,
           ln1_g, ln1_b, ffn_w_up, ffn_conv_w, ffn_conv_b, ffn_w_down, ln2_g, ln2_b):
    for l in range(DEPTH):
        x = _layer(x, w_in[l], na_rpb[l], gdn_conv_w[l], gdn_a_log[l], gdn_dt_bias[l], gdn_norm_w[l],
                   w_out[l], ln1_g[l], ln1_b[l], ffn_w_up[l], ffn_conv_w[l], ffn_conv_b[l],
                   ffn_w_down[l], ln2_g[l], ln2_b[l])
    return x


def setup_inputs(seed: int = 0) -> dict:
    key = jax.random.key(seed)
    ks = jax.random.split(key, 20)
    nrm = lambda k, shp: jax.random.normal(k, shp, jnp.float32)
    x_prompt = nrm(ks[0], (BATCH, SEQ, D_MODEL))
    x_sample = nrm(ks[1], (DEC_BATCH, DEC_SEQ, D_MODEL))
    col_scale = np.ones((IN_COLS,), np.float32)
    offs = np.concatenate([[0], np.cumsum(IN_SPLITS)])
    col_scale[offs[2]:offs[3]] = DEEPNORM_BETA
    col_scale[offs[5]:offs[6]] = DEEPNORM_BETA
    w_in = nrm(ks[2], (DEPTH, D_MODEL, IN_COLS)) * D_MODEL ** -0.5 * jnp.asarray(col_scale)
    na_rpb = 0.02 * nrm(ks[3], (DEPTH, NA_HEADS, 2 * NA_WIN_ROWS - 1, 2 * NA_WIN_COLS - 1))
    gdn_conv_w = nrm(ks[4], (DEPTH, GDN_CONV, GDN_CONV_CH)) * GDN_CONV ** -0.5
    gdn_a_log = jnp.log(jax.random.uniform(ks[5], (DEPTH, 2, GDN_HEADS), jnp.float32, 1.0, 16.0))
    dt = jnp.exp(jax.random.uniform(ks[6], (DEPTH, 2, GDN_HEADS), jnp.float32,
                                    float(np.log(1e-3)), float(np.log(0.1))))
    gdn_dt_bias = dt + jnp.log(-jnp.expm1(-dt))
    gdn_norm_w = 1.0 + 0.02 * nrm(ks[7], (DEPTH, GDN_DV))
    w_out = nrm(ks[8], (DEPTH, MIX_W, D_MODEL)) * MIX_W ** -0.5 * DEEPNORM_BETA
    ln1_g = 1.0 + 0.02 * nrm(ks[9], (DEPTH, D_MODEL))
    ln1_b = 0.02 * nrm(ks[10], (DEPTH, D_MODEL))
    ffn_w_up = nrm(ks[11], (DEPTH, D_MODEL, 2 * D_FF)) * D_MODEL ** -0.5 * DEEPNORM_BETA
    ffn_conv_w = nrm(ks[12], (DEPTH, FFN_CONV, 2 * D_FF)) * FFN_CONV ** -0.5
    ffn_conv_b = 0.02 * nrm(ks[13], (DEPTH, 2 * D_FF))
    ffn_w_down = nrm(ks[14], (DEPTH, D_FF, D_MODEL)) * D_FF ** -0.5 * DEEPNORM_BETA
    ln2_g = 1.0 + 0.02 * nrm(ks[15], (DEPTH, D_MODEL))
    ln2_b = 0.02 * nrm(ks[16], (DEPTH, D_MODEL))
    return {'x_prompt': x_prompt, 'x_sample': x_sample, 'w_in': w_in, 'na_rpb': na_rpb,
            'gdn_conv_w': gdn_conv_w, 'gdn_a_log': gdn_a_log, 'gdn_dt_bias': gdn_dt_bias,
            'gdn_norm_w': gdn_norm_w, 'w_out': w_out, 'ln1_g': ln1_g, 'ln1_b': ln1_b,
            'ffn_w_up': ffn_w_up, 'ffn_conv_w': ffn_conv_w, 'ffn_conv_b': ffn_conv_b,
            'ffn_w_down': ffn_w_down, 'ln2_g': ln2_g, 'ln2_b': ln2_b}


def reference(x_prompt, x_sample, w_in, na_rpb, gdn_conv_w, gdn_a_log, gdn_dt_bias, gdn_norm_w,
              w_out, ln1_g, ln1_b, ffn_w_up, ffn_conv_w, ffn_conv_b, ffn_w_down, ln2_g, ln2_b):
    y_prompt = _trunk(x_prompt, w_in, na_rpb, gdn_conv_w, gdn_a_log, gdn_dt_bias, gdn_norm_w, w_out,
                      ln1_g, ln1_b, ffn_w_up, ffn_conv_w, ffn_conv_b, ffn_w_down, ln2_g, ln2_b)
    y_sample = _trunk(x_sample, w_in, na_rpb, gdn_conv_w, gdn_a_log, gdn_dt_bias, gdn_norm_w, w_out,
                      ln1_g, ln1_b, ffn_w_up, ffn_conv_w, ffn_conv_b, ffn_w_down, ln2_g, ln2_b)
    return (y_prompt, y_sample)
```

```python
import functools

import jax
import jax.numpy as jnp
import numpy as np
from jax import lax
from jax.experimental import pallas as pl
from jax.experimental.pallas import tpu as pltpu

F32 = jnp.float32
BF16 = jnp.bfloat16

D_MODEL = 1024
GRID_W = 64
NA_HEADS = 8
NA_HEAD_DIM = 64
NA_WIN_ROWS = 8
NA_WIN_COLS = 16
GDN_HEADS = 4
GDN_DK = 128
GDN_DV = 128
GDN_CONV = 5
GDN_CHUNK = 64
D_FF = 2816
FFN_CONV = 3
LN_EPS = 1e-5
RMS_EPS = 1e-6

NA_WIDTH = NA_HEADS * NA_HEAD_DIM
GDN_WIDTH = GDN_HEADS * GDN_DK
NA_COLS = 3 * NA_WIDTH
GDN_COLS = 4 * GDN_WIDTH
N_GATES = 4 * GDN_HEADS
IN_COLS = NA_COLS + GDN_COLS + N_GATES

LANES = 128
SUBLANES = 8
BF16_ROWS = 16
VMEM_LIMIT_CAP = 56 * 1024 * 1024

IN_COLS_PAD = NA_COLS + GDN_COLS + LANES
FF_CHUNK = 256
N_FF_CHUNKS = D_FF // FF_CHUNK
NA_SCALE = NA_HEAD_DIM ** -0.5
GDN_Q_SCALE = GDN_DK ** -0.5

LANE_BETA_F, LANE_BETA_B, LANE_G_F, LANE_G_B = 0, GDN_HEADS, 2 * GDN_HEADS, 3 * GDN_HEADS


def _vmem_limit(nbytes):
    return int(min(VMEM_LIMIT_CAP, max(16 * 1024 * 1024, nbytes)))


def _sigmoid(x):
    return 1.0 / (1.0 + jnp.exp(-x))


def _layer_norm_rows(y, g, b):
    mu = jnp.mean(y, axis=-1, keepdims=True)
    yc = y - mu
    var = jnp.mean(yc * yc, axis=-1, keepdims=True)
    return yc * lax.rsqrt(var + LN_EPS) * g + b


IN_PROJ_ROWS = 512
IN_PROJ_COL_CHUNK = 512


def _in_proj_kernel(x_ref, w_ref, na_ref, g_ref, gate_ref):
    xb = x_ref[...].astype(BF16)
    for c in range(NA_COLS // IN_PROJ_COL_CHUNK):
        lo = c * IN_PROJ_COL_CHUNK
        acc = jnp.dot(xb, w_ref[:, lo:lo + IN_PROJ_COL_CHUNK], preferred_element_type=F32)
        if lo < NA_WIDTH:
            acc = acc * NA_SCALE
        na_ref[:, lo:lo + IN_PROJ_COL_CHUNK] = acc.astype(BF16)
    for c in range(GDN_COLS // IN_PROJ_COL_CHUNK):
        lo = c * IN_PROJ_COL_CHUNK
        g_ref[:, lo:lo + IN_PROJ_COL_CHUNK] = jnp.dot(
            xb, w_ref[:, NA_COLS + lo:NA_COLS + lo + IN_PROJ_COL_CHUNK], preferred_element_type=F32)
    gate_ref[...] = jnp.dot(xb, w_ref[:, NA_COLS + GDN_COLS:], preferred_element_type=F32)


def _in_proj(x2d, w_pad):
    t = x2d.shape[0]
    tm = IN_PROJ_ROWS
    vmem = 2 * (tm * D_MODEL * 4 + D_MODEL * IN_COLS_PAD * 2 + tm * NA_COLS * 2 + tm * GDN_COLS * 4
                + tm * LANES * 4) + 4 * tm * IN_PROJ_COL_CHUNK * 4
    return pl.pallas_call(
        _in_proj_kernel,
        grid=(t // tm,),
        in_specs=[pl.BlockSpec((tm, D_MODEL), lambda i: (i, 0)),
                  pl.BlockSpec((D_MODEL, IN_COLS_PAD), lambda i: (0, 0))],
        out_specs=[pl.BlockSpec((tm, NA_COLS), lambda i: (i, 0)),
                   pl.BlockSpec((tm, GDN_COLS), lambda i: (i, 0)),
                   pl.BlockSpec((tm, LANES), lambda i: (i, 0))],
        out_shape=[jax.ShapeDtypeStruct((t, NA_COLS), BF16),
                   jax.ShapeDtypeStruct((t, GDN_COLS), F32),
                   jax.ShapeDtypeStruct((t, LANES), F32)],
        compiler_params=pltpu.CompilerParams(dimension_semantics=("arbitrary",),
                                             vmem_limit_bytes=_vmem_limit(vmem)),
        name="in_proj",
    )(x2d, w_pad)


NA_KEYS = NA_WIN_ROWS * GRID_W


def _na_kernel(q_ref, k_ref, v_ref, bias_ref, o_ref, *, rows):
    lane = lax.broadcasted_iota(jnp.int32, (GRID_W, LANES), 1)
    first = lane < NA_HEAD_DIM
    win = min(NA_WIN_ROWS, rows)

    def row(r, carry):
        rs = jnp.clip(r - win // 2, 0, rows - win)
        dr0 = rs - r + (NA_WIN_ROWS - 1)
        q = q_ref[pl.ds(pl.multiple_of(r * GRID_W, GRID_W), GRID_W), :]
        k = k_ref[pl.ds(pl.multiple_of(rs * GRID_W, GRID_W), NA_KEYS), :]
        v = v_ref[pl.ds(pl.multiple_of(rs * GRID_W, GRID_W), NA_KEYS), :]
        outs = []
        for hh in range(2):
            keep = first if hh == 0 else jnp.logical_not(first)
            qm = jnp.where(keep, q, jnp.zeros_like(q))
            s = lax.dot_general(qm, k, (((1,), (1,)), ((), ())), preferred_element_type=F32)
            s = s + bias_ref[hh, dr0]
            m = jnp.max(s, axis=-1, keepdims=True)
            e = jnp.exp(s - m)
            p = e / jnp.sum(e, axis=-1, keepdims=True)
            outs.append(jnp.dot(p.astype(BF16), v, preferred_element_type=F32))
        o = jnp.where(first, outs[0], outs[1])
        o_ref[pl.ds(pl.multiple_of(r * GRID_W, GRID_W), GRID_W), :] = o.astype(BF16)
        return carry

    lax.fori_loop(0, rows, row, 0)


def _na_attention(na_qkv, bias_tbl, batch, seq):
    rows = seq // GRID_W
    assert rows >= NA_WIN_ROWS
    pairs = NA_HEADS // 2
    blk = (seq, LANES)
    vmem = 2 * (4 * seq * LANES * 2 + 2 * NA_WIN_ROWS * GRID_W * NA_KEYS * 4) + 16 * GRID_W * NA_KEYS * 4
    return pl.pallas_call(
        functools.partial(_na_kernel, rows=rows),
        grid=(batch, pairs),
        in_specs=[pl.BlockSpec(blk, lambda b, p: (b, p)),
                  pl.BlockSpec(blk, lambda b, p: (b, pairs + p)),
                  pl.BlockSpec(blk, lambda b, p: (b, 2 * pairs + p)),
                  pl.BlockSpec((2, NA_WIN_ROWS, GRID_W, NA_KEYS), lambda b, p: (p, 0, 0, 0))],
        out_specs=pl.BlockSpec(blk, lambda b, p: (b, p)),
        out_shape=jax.ShapeDtypeStruct((batch * seq, NA_WIDTH), BF16),
        compiler_params=pltpu.CompilerParams(dimension_semantics=("arbitrary", "arbitrary"),
                                             vmem_limit_bytes=_vmem_limit(vmem)),
        name="na_attn",
    )(na_qkv, na_qkv, na_qkv, bias_tbl)


def _na_bias_table(rpb):
    c = np.arange(GRID_W)[:, None]
    kc = np.arange(GRID_W)[None, :]
    start = np.clip(c - NA_WIN_COLS // 2, 0, GRID_W - NA_WIN_COLS)
    mask = (kc >= start) & (kc < start + NA_WIN_COLS)
    dc = np.clip(kc - c + NA_WIN_COLS - 1, 0, 2 * NA_WIN_COLS - 2)
    colb = jnp.where(jnp.asarray(mask), rpb[:, :, dc].astype(F32), -1e30)
    dr = np.arange(NA_WIN_ROWS)[:, None] + np.arange(NA_WIN_ROWS)[None, :]
    tbl = colb[:, dr]
    return tbl.transpose(0, 1, 3, 2, 4).reshape(NA_HEADS, NA_WIN_ROWS, GRID_W, NA_KEYS)


GDN_PREP_ROWS = 256
HALO = SUBLANES


def _neumann_inverse(n):
    c = GDN_CHUNK
    eye = (lax.broadcasted_iota(jnp.int32, (c, c), 0) == lax.broadcasted_iota(jnp.int32, (c, c), 1)).astype(F32)
    s = eye + n
    y = jnp.dot(n, n, preferred_element_type=F32, precision=lax.Precision.HIGHEST)
    steps = int(np.log2(c)) - 1
    for i in range(steps):
        last = i == steps - 1
        lhs = s if last else jnp.concatenate([s, y], axis=0)
        prod = jnp.dot(lhs, y, preferred_element_type=F32, precision=lax.Precision.HIGHEST)
        s = s + prod[:c]
        if not last:
            y = prod[c:]
    return s


def _gdn_chunk(n, backward, q_s, k_s, v_s, gt_s, state):
    c = GDN_CHUNK
    r0 = pl.multiple_of(n * c, c)
    q = q_s[pl.ds(r0, c), :]
    k = k_s[pl.ds(r0, c), :]
    v = v_s[pl.ds(r0, c), :]
    gt = gt_s[pl.ds(r0, c), :]
    lane_b = LANE_BETA_B if backward else LANE_BETA_F
    lane_g = LANE_G_B if backward else LANE_G_F
    beta = gt[:, lane_b:lane_b + 1]
    gc = gt[:, lane_g:lane_g + 1]
    gc_row = gt.T[lane_g:lane_g + 1, :]
    g_end = gc[0:1, :] if backward else gc[c - 1:c, :]
    ri = lax.broadcasted_iota(jnp.int32, (c, c), 0)
    ci = lax.broadcasted_iota(jnp.int32, (c, c), 1)
    incl = (ci >= ri) if backward else (ci <= ri)
    strict = (ci > ri) if backward else (ci < ri)
    decay = jnp.where(incl, jnp.exp(jnp.where(incl, gc - gc_row, 0.0)), 0.0)
    kb = k * beta
    kq = lax.dot_general(jnp.concatenate([kb, q], axis=0), k, (((1,), (1,)), ((), ())),
                         preferred_element_type=F32)
    neg_l = -jnp.where(strict, kq[:c] * decay, 0.0)
    attn = kq[c:] * decay
    t_inv = _neumann_inverse(neg_l)
    eg = jnp.exp(gc)
    rhs = jnp.concatenate([v * beta, kb * eg], axis=1)
    sol = jnp.dot(t_inv, rhs, preferred_element_type=F32, precision=lax.Precision.HIGHEST)
    u = sol[:, :GDN_DV]
    w = sol[:, GDN_DV:]
    qd = q * eg
    kd = k * jnp.exp(g_end - gc)
    ws = jnp.dot(jnp.concatenate([w, qd], axis=0), state, preferred_element_type=F32)
    v_new = u - ws[:c]
    o = ws[c:] + jnp.dot(attn, v_new, preferred_element_type=F32)
    state = state * jnp.exp(g_end) + lax.dot_general(kd, v_new, (((0,), (0,)), ((), ())),
                                                     preferred_element_type=F32)
    return o, state


def _gdn_kernel(q_ref, k_ref, v_ref, z_ref, gate_ref, cwq_ref, cwk_ref, cwv_ref, alog_ref, dtb_ref, nw_ref,
                o_ref, xp_s, qn_s, kn_s, vn_s, gt_s, of_s, ob_s, st_s, *, seq):
    head = pl.program_id(1)
    c = GDN_CHUNK
    n_chunks = seq // c
    rt = GDN_PREP_ROWS
    n_tiles = seq // rt

    lane = lax.broadcasted_iota(jnp.int32, (rt, LANES), 1)
    ri = lax.broadcasted_iota(jnp.int32, (rt, LANES), 0) % c
    shift = (LANES - head) % LANES
    for t in range(n_tiles):
        x = gate_ref[t * rt:(t + 1) * rt, :]
        a = x + dtb_ref[...]
        softplus = jnp.maximum(a, 0.0) + jnp.log(1.0 + jnp.exp(-jnp.abs(a)))
        g = -jnp.exp(alog_ref[...]) * softplus
        pre = g
        suf = g
        s = 1
        while s < c:
            pre = pre + jnp.where(ri >= s, pltpu.roll(pre, s, axis=0), 0.0)
            suf = suf + jnp.where(ri < c - s, pltpu.roll(suf, rt - s, axis=0), 0.0)
            s *= 2
        is_fwd = (lane >= 2 * GDN_HEADS) & (lane < 3 * GDN_HEADS)
        is_bwd = (lane >= 3 * GDN_HEADS) & (lane < 4 * GDN_HEADS)
        val = jnp.where(is_fwd, pre, jnp.where(is_bwd, suf, _sigmoid(x)))
        gt_s[t * rt:(t + 1) * rt, :] = pltpu.roll(val, shift, axis=1)

    zero_halo = jnp.zeros((HALO, LANES), F32)
    xp_s[0:HALO, :] = zero_halo
    xp_s[HALO + seq:2 * HALO + seq, :] = zero_halo
    pad = GDN_CONV // 2
    for src_ref, cw_ref, dst_s, norm_scale in ((q_ref, cwq_ref, qn_s, GDN_Q_SCALE),
                                               (k_ref, cwk_ref, kn_s, 1.0),
                                               (v_ref, cwv_ref, vn_s, None)):
        for t in range(n_tiles):
            xp_s[HALO + t * rt:HALO + (t + 1) * rt, :] = src_ref[t * rt:(t + 1) * rt, :]
        for t in range(n_tiles):
            base = HALO + t * rt - pad
            y = xp_s[base:base + rt, :] * cw_ref[0:1, :]
            for j in range(1, GDN_CONV):
                y = y + xp_s[base + j:base + j + rt, :] * cw_ref[j:j + 1, :]
            y = y * _sigmoid(y)
            if norm_scale is not None:
                y = y * lax.rsqrt(jnp.sum(y * y, axis=-1, keepdims=True) + RMS_EPS)
                if norm_scale != 1.0:
                    y = y * norm_scale
            dst_s[t * rt:(t + 1) * rt, :] = y

    st_s[...] = jnp.zeros_like(st_s)

    def step(i, carry):
        nf = i
        nb = n_chunks - 1 - i
        o_f, s_f = _gdn_chunk(nf, False, qn_s, kn_s, vn_s, gt_s, st_s[0])
        of_s[pl.ds(pl.multiple_of(nf * c, c), c), :] = o_f
        st_s[0] = s_f
        o_b, s_b = _gdn_chunk(nb, True, qn_s, kn_s, vn_s, gt_s, st_s[1])
        ob_s[pl.ds(pl.multiple_of(nb * c, c), c), :] = o_b
        st_s[1] = s_b
        return carry

    lax.fori_loop(0, n_chunks, step, 0)

    for t in range(n_tiles):
        sl = slice(t * rt, (t + 1) * rt)
        o = of_s[sl, :] + ob_s[sl, :]
        o = o * lax.rsqrt(jnp.mean(o * o, axis=-1, keepdims=True) + RMS_EPS) * nw_ref[...]
        z = z_ref[sl, :]
        o_ref[sl, :] = (o * (z * _sigmoid(z))).astype(BF16)


def _gdn(g_qkvz, gates, conv_w, alog_lane, dtb_lane, norm_w, batch, seq):
    assert seq % GDN_PREP_ROWS == 0 and GDN_PREP_ROWS % GDN_CHUNK == 0
    h = GDN_HEADS
    blk = (seq, LANES)
    cw_blk = (GDN_CONV, LANES)
    row_blk = (1, LANES)
    vmem = 2 * (5 * seq * LANES * 4 + seq * LANES * 2) + (7 * seq + 2 * HALO) * LANES * 4 + 8 * 1024 * 1024
    return pl.pallas_call(
        functools.partial(_gdn_kernel, seq=seq),
        grid=(batch, h),
        in_specs=[pl.BlockSpec(blk, lambda b, i: (b, i)),
                  pl.BlockSpec(blk, lambda b, i: (b, h + i)),
                  pl.BlockSpec(blk, lambda b, i: (b, 2 * h + i)),
                  pl.BlockSpec(blk, lambda b, i: (b, 3 * h + i)),
                  pl.BlockSpec(blk, lambda b, i: (b, 0)),
                  pl.BlockSpec(cw_blk, lambda b, i: (0, i)),
                  pl.BlockSpec(cw_blk, lambda b, i: (0, h + i)),
                  pl.BlockSpec(cw_blk, lambda b, i: (0, 2 * h + i)),
                  pl.BlockSpec(row_blk, lambda b, i: (0, 0)),
                  pl.BlockSpec(row_blk, lambda b, i: (0, 0)),
                  pl.BlockSpec(row_blk, lambda b, i: (0, 0))],
        out_specs=pl.BlockSpec(blk, lambda b, i: (b, i)),
        out_shape=jax.ShapeDtypeStruct((batch * seq, GDN_WIDTH), BF16),
        scratch_shapes=[pltpu.VMEM((seq + 2 * HALO, LANES), F32),
                        pltpu.VMEM((seq, LANES), F32), pltpu.VMEM((seq, LANES), F32),
                        pltpu.VMEM((seq, LANES), F32), pltpu.VMEM((seq, LANES), F32),
                        pltpu.VMEM((seq, LANES), F32), pltpu.VMEM((seq, LANES), F32),
                        pltpu.VMEM((2, GDN_DK, GDN_DV), F32)],
        compiler_params=pltpu.CompilerParams(dimension_semantics=("arbitrary", "arbitrary"),
                                             vmem_limit_bytes=_vmem_limit(vmem)),
        name="gdn",
    )(g_qkvz, g_qkvz, g_qkvz, g_qkvz, gates, conv_w, conv_w, conv_w, alog_lane, dtb_lane, norm_w)


OUT_PROJ_ROWS = 512


def _out_proj_kernel(na_ref, gd_ref, x_ref, w_ref, g_ref, b_ref, o_ref, *, alpha):
    mix = jnp.dot(na_ref[...], w_ref[0:NA_WIDTH, :], preferred_element_type=F32)
    mix = mix + jnp.dot(gd_ref[...], w_ref[NA_WIDTH:, :], preferred_element_type=F32)
    o_ref[...] = _layer_norm_rows(alpha * x_ref[...] + mix, g_ref[...], b_ref[...])


def _out_proj(na_out, gdn_out, x2d, w_out, ln_g, ln_b, alpha):
    t = x2d.shape[0]
    tm = OUT_PROJ_ROWS
    vmem = 2 * (2 * tm * NA_WIDTH * 2 + 2 * tm * D_MODEL * 4 + D_MODEL * D_MODEL * 2) + 4 * tm * D_MODEL * 4
    return pl.pallas_call(
        functools.partial(_out_proj_kernel, alpha=alpha),
        grid=(t // tm,),
        in_specs=[pl.BlockSpec((tm, NA_WIDTH), lambda i: (i, 0)),
                  pl.BlockSpec((tm, GDN_WIDTH), lambda i: (i, 0)),
                  pl.BlockSpec((tm, D_MODEL), lambda i: (i, 0)),
                  pl.BlockSpec((NA_WIDTH + GDN_WIDTH, D_MODEL), lambda i: (0, 0)),
                  pl.BlockSpec((1, D_MODEL), lambda i: (0, 0)),
                  pl.BlockSpec((1, D_MODEL), lambda i: (0, 0))],
        out_specs=pl.BlockSpec((tm, D_MODEL), lambda i: (i, 0)),
        out_shape=jax.ShapeDtypeStruct((t, D_MODEL), F32),
        compiler_params=pltpu.CompilerParams(dimension_semantics=("arbitrary",),
                                             vmem_limit_bytes=_vmem_limit(vmem)),
        name="out_proj",
    )(na_out, gdn_out, x2d, w_out, ln_g, ln_b)


FFN_SUB_ROWS = 1024
FFN_HALO = BF16_ROWS
FFN_LN_ROWS = 256


def _ffn_kernel(x_ref, wup_ref, cw_ref, cb_ref, wdn_ref, g_ref, b_ref, o_ref, xb_s, *, seq, alpha):
    c = pl.program_id(1)
    sub = min(FFN_SUB_ROWS, seq)

    @pl.when(c == 0)
    def _init():
        zero_halo = jnp.zeros((FFN_HALO, D_MODEL), BF16)
        xb_s[0:FFN_HALO, :] = zero_halo
        xb_s[FFN_HALO + seq:2 * FFN_HALO + seq, :] = zero_halo
        for t in range(seq // FFN_LN_ROWS):
            sl = slice(t * FFN_LN_ROWS, (t + 1) * FFN_LN_ROWS)
            x = x_ref[sl, :]
            xb_s[FFN_HALO + t * FFN_LN_ROWS:FFN_HALO + (t + 1) * FFN_LN_ROWS, :] = x.astype(BF16)
            o_ref[sl, :] = alpha * x

    cw = cw_ref[0]
    for rb in range(seq // sub):
        r0 = rb * sub
        xin = xb_s[r0:r0 + sub + 2 * FFN_HALO, :]
        h = jnp.dot(xin, wup_ref[0], preferred_element_type=F32)
        hc = cb_ref[0] + cw[0:1, :] * h[FFN_HALO - 1:FFN_HALO - 1 + sub, :]
        hc = hc + cw[1:2, :] * h[FFN_HALO:FFN_HALO + sub, :]
        hc = hc + cw[2:3, :] * h[FFN_HALO + 1:FFN_HALO + 1 + sub, :]
        gate = hc[:, :FF_CHUNK]
        act = (gate * _sigmoid(gate) * hc[:, FF_CHUNK:]).astype(BF16)
        o_ref[r0:r0 + sub, :] += jnp.dot(act, wdn_ref[0], preferred_element_type=F32)

    @pl.when(c == pl.num_programs(1) - 1)
    def _finish():
        for t in range(seq // FFN_LN_ROWS):
            sl = slice(t * FFN_LN_ROWS, (t + 1) * FFN_LN_ROWS)
            o_ref[sl, :] = _layer_norm_rows(o_ref[sl, :], g_ref[...], b_ref[...])


def _ffn(x1, wup_c, cw_c, cb_c, wdn_c, ln_g, ln_b, batch, seq, alpha):
    assert seq % FFN_LN_ROWS == 0 and seq % min(FFN_SUB_ROWS, seq) == 0
    sub = min(FFN_SUB_ROWS, seq)
    vmem = (4 * seq * D_MODEL * 4 + (seq + 2 * FFN_HALO) * D_MODEL * 2
            + 2 * (D_MODEL * 2 * FF_CHUNK * 2 + FF_CHUNK * D_MODEL * 2)
            + (sub + 2 * FFN_HALO) * 2 * FF_CHUNK * 4 * 3 + sub * D_MODEL * 4)
    return pl.pallas_call(
        functools.partial(_ffn_kernel, seq=seq, alpha=alpha),
        grid=(batch, N_FF_CHUNKS),
        in_specs=[pl.BlockSpec((seq, D_MODEL), lambda b, c: (b, 0)),
                  pl.BlockSpec((1, D_MODEL, 2 * FF_CHUNK), lambda b, c: (c, 0, 0)),
                  pl.BlockSpec((1, FFN_CONV, 2 * FF_CHUNK), lambda b, c: (c, 0, 0)),
                  pl.BlockSpec((1, 1, 2 * FF_CHUNK), lambda b, c: (c, 0, 0)),
                  pl.BlockSpec((1, FF_CHUNK, D_MODEL), lambda b, c: (c, 0, 0)),
                  pl.BlockSpec((1, D_MODEL), lambda b, c: (0, 0)),
                  pl.BlockSpec((1, D_MODEL), lambda b, c: (0, 0))],
        out_specs=pl.BlockSpec((seq, D_MODEL), lambda b, c: (b, 0)),
        out_shape=jax.ShapeDtypeStruct((batch * seq, D_MODEL), F32),
        scratch_shapes=[pltpu.VMEM((seq + 2 * FFN_HALO, D_MODEL), BF16)],
        compiler_params=pltpu.CompilerParams(dimension_semantics=("arbitrary", "arbitrary"),
                                             vmem_limit_bytes=_vmem_limit(vmem)),
        name="ffn",
    )(x1, wup_c, cw_c, cb_c, wdn_c, ln_g, ln_b)


def _chunk_ff(w, axis):
    w = jnp.moveaxis(w, axis, -1)
    lead = w.shape[:-1]
    w = w.reshape(lead + (2, N_FF_CHUNKS, FF_CHUNK))
    w = jnp.moveaxis(w, -2, 0)
    return w.reshape((N_FF_CHUNKS,) + lead + (2 * FF_CHUNK,))


def _prep_layer_params(w_in, na_rpb, gdn_conv_w, gdn_a_log, gdn_dt_bias, gdn_norm_w, w_out,
                       ln1_g, ln1_b, ffn_w_up, ffn_conv_w, ffn_conv_b, ffn_w_down, ln2_g, ln2_b):
    w_pad = jnp.pad(w_in, ((0, 0), (0, IN_COLS_PAD - IN_COLS))).astype(BF16)
    gate_pad = (2 * GDN_HEADS, LANES - N_GATES)
    alog_lane = jnp.pad(gdn_a_log.astype(F32).reshape(-1), gate_pad).reshape(1, LANES)
    dtb_lane = jnp.pad(gdn_dt_bias.astype(F32).reshape(-1), gate_pad).reshape(1, LANES)
    return dict(
        w_pad=w_pad,
        bias_tbl=_na_bias_table(na_rpb),
        conv_w=gdn_conv_w.astype(F32),
        alog_lane=alog_lane,
        dtb_lane=dtb_lane,
        norm_w=gdn_norm_w.astype(F32).reshape(1, GDN_DV),
        w_out=w_out.astype(BF16),
        ln1_g=ln1_g.astype(F32).reshape(1, D_MODEL), ln1_b=ln1_b.astype(F32).reshape(1, D_MODEL),
        wup_c=_chunk_ff(ffn_w_up, 1).astype(BF16),
        cw_c=_chunk_ff(ffn_conv_w.astype(F32), 1),
        cb_c=_chunk_ff(ffn_conv_b.astype(F32).reshape(1, -1), 1),
        wdn_c=ffn_w_down.astype(BF16).reshape(N_FF_CHUNKS, FF_CHUNK, D_MODEL),
        ln2_g=ln2_g.astype(F32).reshape(1, D_MODEL), ln2_b=ln2_b.astype(F32).reshape(1, D_MODEL),
    )


def _layer(x2d, p, batch, seq, alpha):
    na_qkv, g_qkvz, gates = _in_proj(x2d, p["w_pad"])
    na_out = _na_attention(na_qkv, p["bias_tbl"], batch, seq)
    gdn_out = _gdn(g_qkvz, gates, p["conv_w"], p["alog_lane"], p["dtb_lane"], p["norm_w"], batch, seq)
    x1 = _out_proj(na_out, gdn_out, x2d, p["w_out"], p["ln1_g"], p["ln1_b"], alpha)
    return _ffn(x1, p["wup_c"], p["cw_c"], p["cb_c"], p["wdn_c"], p["ln2_g"], p["ln2_b"], batch, seq, alpha)


def _trunk(x, layer_params, alpha):
    batch, seq, d = x.shape
    x2d = x.reshape(batch * seq, d)
    for p in layer_params:
        x2d = _layer(x2d, p, batch, seq, alpha)
    return x2d.reshape(batch, seq, d)


def kernel(x_prompt, x_sample, w_in, na_rpb, gdn_conv_w, gdn_a_log, gdn_dt_bias, gdn_norm_w, w_out, ln1_g, ln1_b,
           ffn_w_up, ffn_conv_w, ffn_conv_b, ffn_w_down, ln2_g, ln2_b):
    depth = w_in.shape[0]
    alpha = float((2 * depth) ** 0.25)
    stacked = (w_in, na_rpb, gdn_conv_w, gdn_a_log, gdn_dt_bias, gdn_norm_w, w_out, ln1_g, ln1_b,
               ffn_w_up, ffn_conv_w, ffn_conv_b, ffn_w_down, ln2_g, ln2_b)
    layer_params = [_prep_layer_params(*(a[l] for a in stacked)) for l in range(depth)]
    return (_trunk(x_prompt, layer_params, alpha), _trunk(x_sample, layer_params, alpha))
```

```python
import functools

import jax
import jax.numpy as jnp
import numpy as np
from jax import lax
from jax.experimental import pallas as pl
from jax.experimental.pallas import tpu as pltpu

F32 = jnp.float32
BF16 = jnp.bfloat16

D_MODEL = 1024
GRID_W = 64
NA_HEADS = 8
NA_HEAD_DIM = 64
NA_WIN_ROWS = 8
NA_WIN_COLS = 16
GDN_HEADS = 4
GDN_DK = 128
GDN_DV = 128
GDN_CONV = 5
GDN_CHUNK = 64
D_FF = 2816
FFN_CONV = 3
LN_EPS = 1e-5
RMS_EPS = 1e-6

NA_WIDTH = NA_HEADS * NA_HEAD_DIM
GDN_WIDTH = GDN_HEADS * GDN_DK
NA_COLS = 3 * NA_WIDTH
GDN_COLS = 4 * GDN_WIDTH
N_GATES = 4 * GDN_HEADS
IN_COLS = NA_COLS + GDN_COLS + N_GATES

LANES = 128
SUBLANES = 8
BF16_ROWS = 16
VMEM_LIMIT_CAP = 56 * 1024 * 1024

IN_COLS_PAD = NA_COLS + GDN_COLS + LANES
FF_CHUNK = 256
N_FF_CHUNKS = D_FF // FF_CHUNK
NA_SCALE = NA_HEAD_DIM ** -0.5
GDN_Q_SCALE = GDN_DK ** -0.5

LANE_BETA_F, LANE_BETA_B, LANE_G_F, LANE_G_B = 0, GDN_HEADS, 2 * GDN_HEADS, 3 * GDN_HEADS


def _vmem_limit(nbytes):
    return int(min(VMEM_LIMIT_CAP, max(16 * 1024 * 1024, nbytes)))


def _sigmoid(x):
    return 1.0 / (1.0 + jnp.exp(-x))


def _layer_norm_rows(y, g, b):
    mu = jnp.mean(y, axis=-1, keepdims=True)
    yc = y - mu
    var = jnp.mean(yc * yc, axis=-1, keepdims=True)
    return yc * lax.rsqrt(var + LN_EPS) * g + b


IN_PROJ_ROWS = 512
IN_PROJ_COL_CHUNK = 512


def _in_proj_kernel(x_ref, w_ref, na_ref, g_ref, gate_ref):
    xb = x_ref[...].astype(BF16)
    for c in range(NA_COLS // IN_PROJ_COL_CHUNK):
        lo = c * IN_PROJ_COL_CHUNK
        acc = jnp.dot(xb, w_ref[:, lo:lo + IN_PROJ_COL_CHUNK], preferred_element_type=F32)
        if lo < NA_WIDTH:
            acc = acc * NA_SCALE
        na_ref[:, lo:lo + IN_PROJ_COL_CHUNK] = acc.astype(BF16)
    for c in range(GDN_COLS // IN_PROJ_COL_CHUNK):
        lo = c * IN_PROJ_COL_CHUNK
        g_ref[:, lo:lo + IN_PROJ_COL_CHUNK] = jnp.dot(
            xb, w_ref[:, NA_COLS + lo:NA_COLS + lo + IN_PROJ_COL_CHUNK], preferred_element_type=F32)
    gate_ref[...] = jnp.dot(xb, w_ref[:, NA_COLS + GDN_COLS:], preferred_element_type=F32)


def _in_proj(x2d, w_pad):
    t = x2d.shape[0]
    tm = IN_PROJ_ROWS
    vmem = 2 * (tm * D_MODEL * 4 + D_MODEL * IN_COLS_PAD * 2 + tm * NA_COLS * 2 + tm * GDN_COLS * 4
                + tm * LANES * 4) + 4 * tm * IN_PROJ_COL_CHUNK * 4
    return pl.pallas_call(
        _in_proj_kernel,
        grid=(t // tm,),
        in_specs=[pl.BlockSpec((tm, D_MODEL), lambda i: (i, 0)),
                  pl.BlockSpec((D_MODEL, IN_COLS_PAD), lambda i: (0, 0))],
        out_specs=[pl.BlockSpec((tm, NA_COLS), lambda i: (i, 0)),
                   pl.BlockSpec((tm, GDN_COLS), lambda i: (i, 0)),
                   pl.BlockSpec((tm, LANES), lambda i: (i, 0))],
        out_shape=[jax.ShapeDtypeStruct((t, NA_COLS), BF16),
                   jax.ShapeDtypeStruct((t, GDN_COLS), F32),
                   jax.ShapeDtypeStruct((t, LANES), F32)],
        compiler_params=pltpu.CompilerParams(dimension_semantics=("arbitrary",),
                                             vmem_limit_bytes=_vmem_limit(vmem)),
        name="in_proj",
    )(x2d, w_pad)


NA_KEYS = NA_WIN_ROWS * GRID_W
NA_ROWS_PER_STEP = 8


def _na_kernel(q_ref, k_ref, v_ref, bias_ref, o_ref, *, rows):
    lane = lax.broadcasted_iota(jnp.int32, (GRID_W, LANES), 1)
    first = lane < NA_HEAD_DIM
    win = min(NA_WIN_ROWS, rows)

    def row_group(g, carry):
        chains = []
        for j in range(NA_ROWS_PER_STEP):
            r = g * NA_ROWS_PER_STEP + j
            rs = jnp.clip(r - win // 2, 0, rows - win)
            dr0 = rs - r + (NA_WIN_ROWS - 1)
            q = q_ref[pl.ds(pl.multiple_of(r * GRID_W, GRID_W), GRID_W), :]
            k = k_ref[pl.ds(pl.multiple_of(rs * GRID_W, GRID_W), NA_KEYS), :]
            v = v_ref[pl.ds(pl.multiple_of(rs * GRID_W, GRID_W), NA_KEYS), :]
            for hh in range(2):
                keep = first if hh == 0 else jnp.logical_not(first)
                chains.append(dict(r=r, hh=hh, dr0=dr0, k=k, v=v, qm=jnp.where(keep, q, jnp.zeros_like(q))))
        for ch in chains:
            s = lax.dot_general(ch["qm"], ch["k"], (((1,), (1,)), ((), ())), preferred_element_type=F32)
            ch["s"] = s + bias_ref[ch["hh"], ch["dr0"]]
        for ch in chains:
            ch["m"] = jnp.max(ch["s"], axis=-1, keepdims=True)
        for ch in chains:
            e = jnp.exp(ch["s"] - ch["m"])
            ch["inv"] = 1.0 / jnp.sum(e, axis=-1, keepdims=True)
            ch["e"] = e.astype(BF16)
        for ch in chains:
            ch["o"] = jnp.dot(ch["e"], ch["v"], preferred_element_type=F32) * ch["inv"]
        for j in range(NA_ROWS_PER_STEP):
            o = jnp.where(first, chains[2 * j]["o"], chains[2 * j + 1]["o"])
            r = chains[2 * j]["r"]
            o_ref[pl.ds(pl.multiple_of(r * GRID_W, GRID_W), GRID_W), :] = o.astype(BF16)
        return carry

    lax.fori_loop(0, rows // NA_ROWS_PER_STEP, row_group, 0)


def _na_attention(na_qkv, bias_tbl, batch, seq):
    rows = seq // GRID_W
    assert rows >= NA_WIN_ROWS and rows % NA_ROWS_PER_STEP == 0
    pairs = NA_HEADS // 2
    blk = (seq, LANES)
    vmem = 2 * (4 * seq * LANES * 2 + 2 * NA_WIN_ROWS * GRID_W * NA_KEYS * 4) + 16 * GRID_W * NA_KEYS * 4
    return pl.pallas_call(
        functools.partial(_na_kernel, rows=rows),
        grid=(batch, pairs),
        in_specs=[pl.BlockSpec(blk, lambda b, p: (b, p)),
                  pl.BlockSpec(blk, lambda b, p: (b, pairs + p)),
                  pl.BlockSpec(blk, lambda b, p: (b, 2 * pairs + p)),
                  pl.BlockSpec((2, NA_WIN_ROWS, GRID_W, NA_KEYS), lambda b, p: (p, 0, 0, 0))],
        out_specs=pl.BlockSpec(blk, lambda b, p: (b, p)),
        out_shape=jax.ShapeDtypeStruct((batch * seq, NA_WIDTH), BF16),
        compiler_params=pltpu.CompilerParams(dimension_semantics=("arbitrary", "arbitrary"),
                                             vmem_limit_bytes=_vmem_limit(vmem)),
        name="na_attn",
    )(na_qkv, na_qkv, na_qkv, bias_tbl)


def _na_bias_table(rpb):
    c = np.arange(GRID_W)[:, None]
    kc = np.arange(GRID_W)[None, :]
    start = np.clip(c - NA_WIN_COLS // 2, 0, GRID_W - NA_WIN_COLS)
    mask = (kc >= start) & (kc < start + NA_WIN_COLS)
    dc = np.clip(kc - c + NA_WIN_COLS - 1, 0, 2 * NA_WIN_COLS - 2)
    colb = jnp.where(jnp.asarray(mask), rpb[:, :, dc].astype(F32), -1e30)
    dr = np.arange(NA_WIN_ROWS)[:, None] + np.arange(NA_WIN_ROWS)[None, :]
    tbl = colb[:, dr]
    return tbl.transpose(0, 1, 3, 2, 4).reshape(NA_HEADS, NA_WIN_ROWS, GRID_W, NA_KEYS)


GDN_PREP_ROWS = 256
GDN_PREP_PAIRS = 4
HALO = SUBLANES


def _lane_pair(a, b):
    return jnp.concatenate([a, b], axis=1)


def _block_diag_rows(y, left):
    zero = jnp.zeros_like(y)
    return jnp.concatenate([jnp.where(left, y, zero), jnp.where(left, zero, y)], axis=0)


def _gdn_prepare_group(base, q_s, k_s, v_s, gt_s, mq_s, c_s, o_s, cd_s):
    c = GDN_CHUNK
    lane = lax.broadcasted_iota(jnp.int32, (c, 2 * c), 1)
    ri = lax.broadcasted_iota(jnp.int32, (c, 2 * c), 0)
    left = lane < c
    ci = jnp.where(left, lane, lane - c)
    top = lax.broadcasted_iota(jnp.int32, (2 * c, 1), 0) < c
    incl = [ci <= ri, ci >= ri]
    strict = [ci < ri, ci > ri]
    lane_beta = (LANE_BETA_F, LANE_BETA_B)
    lane_g = (LANE_G_F, LANE_G_B)
    zero_tile = jnp.zeros((c, LANES), F32)
    zero_wide = jnp.zeros((c, GDN_DK + GDN_DV), F32)

    chains = []
    for p in range(GDN_PREP_PAIRS):
        n0 = base + 2 * p
        r0 = pl.multiple_of(n0 * c, 2 * c)
        q2 = q_s[pl.ds(r0, 2 * c), :]
        k2 = k_s[pl.ds(r0, 2 * c), :]
        v2 = v_s[pl.ds(r0, 2 * c), :]
        gt2 = gt_s[pl.ds(r0, 2 * c), :]
        gt2_t = gt2.T
        kb2 = [k2 * gt2[:, lane_beta[d]:lane_beta[d] + 1] for d in range(2)]
        lhs = jnp.concatenate([_lane_pair(kb2[0][:c], kb2[0][c:]), _lane_pair(kb2[1][:c], kb2[1][c:]),
                               _lane_pair(q2[:c], q2[c:])], axis=0)
        k_diag = jnp.concatenate([_lane_pair(k2[:c], zero_tile), _lane_pair(zero_tile, k2[c:])], axis=0)
        kq = lax.dot_general(lhs, k_diag, (((1,), (1,)), ((), ())), preferred_element_type=F32)
        for d in range(2):
            gc2 = gt2[:, lane_g[d]:lane_g[d] + 1]
            gcol = jnp.where(left, gc2[:c], gc2[c:])
            grow = gt2_t[lane_g[d]:lane_g[d] + 1, :]
            decay = jnp.where(incl[d], jnp.exp(jnp.where(incl[d], gcol - grow, 0.0)), 0.0)
            e0 = (0 if d else c - 1)
            g_end = [gc2[e0:e0 + 1], gc2[c + e0:c + e0 + 1]]
            ge2 = jnp.where(top, g_end[0], g_end[1])
            eg2 = jnp.exp(gc2)
            kbe2 = kb2[d] * eg2
            vb2 = v2 * gt2[:, lane_beta[d]:lane_beta[d] + 1]
            chains.append(dict(
                d=d, n0=n0, r0=r0, g_end=g_end,
                neg_l=-jnp.where(strict[d], kq[d * c:(d + 1) * c] * decay, 0.0),
                attn=kq[2 * c:] * decay,
                rhs=[_lane_pair(kbe2[:c], vb2[:c]), _lane_pair(kbe2[c:], vb2[c:])],
                qe2=q2 * eg2,
                kd2=k2 * jnp.exp(ge2 - gc2)))

    for ch in chains:
        ch["p"] = ch["neg_l"]
        ch["y"] = jnp.dot(ch["neg_l"], _block_diag_rows(ch["neg_l"], left), preferred_element_type=F32)
    steps = int(np.log2(c)) - 1
    for i in range(steps):
        last = i == steps - 1
        for ch in chains:
            lhs = ch["p"] if last else jnp.concatenate([ch["p"], ch["y"]], axis=0)
            prod = jnp.dot(lhs, _block_diag_rows(ch["y"], left), preferred_element_type=F32)
            ch["p"] = ch["p"] + ch["y"] + prod[:c]
            if not last:
                ch["y"] = prod[c:]

    for ch in chains:
        ch["sol"] = [ch["rhs"][0] + jnp.dot(ch["p"], jnp.concatenate([ch["rhs"][0], zero_wide], axis=0),
                                            preferred_element_type=F32),
                     ch["rhs"][1] + jnp.dot(ch["p"], jnp.concatenate([zero_wide, ch["rhs"][1]], axis=0),
                                            preferred_element_type=F32)]
    for ch in chains:
        lhs = jnp.concatenate([ch["kd2"].T, ch["attn"]], axis=0)
        ch["big"] = [jnp.dot(lhs, jnp.concatenate([ch["sol"][0], zero_wide], axis=0), preferred_element_type=F32),
                     jnp.dot(lhs, jnp.concatenate([zero_wide, ch["sol"][1]], axis=0), preferred_element_type=F32)]
    for ch in chains:
        d = ch["d"]
        for j in range(2):
            big = ch["big"][j]
            n = ch["n0"] + j
            q_eff = ch["qe2"][j * c:(j + 1) * c] - big[GDN_DK:, :GDN_DK]
            mq_s[d, n] = jnp.concatenate([-big[:GDN_DK, :GDN_DK], q_eff], axis=0)
            c_s[d, n] = big[:GDN_DK, GDN_DK:]
            o_s[d, pl.ds(ch["r0"] + j * c, c), :] = big[GDN_DK:, GDN_DK:]
            cd_s[d, n] = jnp.broadcast_to(jnp.exp(ch["g_end"][j]), (SUBLANES, LANES))


def _gdn_kernel(q_ref, k_ref, v_ref, z_ref, gate_ref, cwq_ref, cwk_ref, cwv_ref, alog_ref, dtb_ref, nw_ref,
                o_ref, xp_s, qn_s, kn_s, vn_s, gt_s, o_s, mq_s, c_s, cd_s, *, seq):
    head = pl.program_id(1)
    c = GDN_CHUNK
    n_chunks = seq // c
    rt = GDN_PREP_ROWS
    n_tiles = seq // rt

    lane = lax.broadcasted_iota(jnp.int32, (rt, LANES), 1)
    ri = lax.broadcasted_iota(jnp.int32, (rt, LANES), 0) % c
    shift = (LANES - head) % LANES
    for t in range(n_tiles):
        x = gate_ref[t * rt:(t + 1) * rt, :]
        a = x + dtb_ref[...]
        softplus = jnp.maximum(a, 0.0) + jnp.log(1.0 + jnp.exp(-jnp.abs(a)))
        g = -jnp.exp(alog_ref[...]) * softplus
        pre = g
        suf = g
        s = 1
        while s < c:
            pre = pre + jnp.where(ri >= s, pltpu.roll(pre, s, axis=0), 0.0)
            suf = suf + jnp.where(ri < c - s, pltpu.roll(suf, rt - s, axis=0), 0.0)
            s *= 2
        is_fwd = (lane >= 2 * GDN_HEADS) & (lane < 3 * GDN_HEADS)
        is_bwd = (lane >= 3 * GDN_HEADS) & (lane < 4 * GDN_HEADS)
        val = jnp.where(is_fwd, pre, jnp.where(is_bwd, suf, _sigmoid(x)))
        gt_s[t * rt:(t + 1) * rt, :] = pltpu.roll(val, shift, axis=1)

    zero_halo = jnp.zeros((HALO, LANES), F32)
    xp_s[0:HALO, :] = zero_halo
    xp_s[HALO + seq:2 * HALO + seq, :] = zero_halo
    pad = GDN_CONV // 2
    for src_ref, cw_ref, dst_s, norm_scale in ((q_ref, cwq_ref, qn_s, GDN_Q_SCALE),
                                               (k_ref, cwk_ref, kn_s, 1.0),
                                               (v_ref, cwv_ref, vn_s, None)):
        for t in range(n_tiles):
            xp_s[HALO + t * rt:HALO + (t + 1) * rt, :] = src_ref[t * rt:(t + 1) * rt, :]
        for t in range(n_tiles):
            base = HALO + t * rt - pad
            y = xp_s[base:base + rt, :] * cw_ref[0:1, :]
            for j in range(1, GDN_CONV):
                y = y + xp_s[base + j:base + j + rt, :] * cw_ref[j:j + 1, :]
            y = y * _sigmoid(y)
            if norm_scale is not None:
                y = y * lax.rsqrt(jnp.sum(y * y, axis=-1, keepdims=True) + RMS_EPS)
                if norm_scale != 1.0:
                    y = y * norm_scale
            dst_s[t * rt:(t + 1) * rt, :] = y

    def prepare(i, carry):
        _gdn_prepare_group(i * (2 * GDN_PREP_PAIRS), qn_s, kn_s, vn_s, gt_s, mq_s, c_s, o_s, cd_s)
        return carry

    lax.fori_loop(0, n_chunks // (2 * GDN_PREP_PAIRS), prepare, 0)

    def scan(i, states):
        new_states = []
        for d, n in ((0, i), (1, n_chunks - 1 - i)):
            s = states[d]
            z = jnp.dot(mq_s[d, n], s, preferred_element_type=F32)
            rows = pl.ds(pl.multiple_of(n * c, c), c)
            o_s[d, rows, :] = o_s[d, rows, :] + z[GDN_DK:]
            new_states.append(s * cd_s[d, n][0:1, :] + z[:GDN_DK] + c_s[d, n])
        return tuple(new_states)

    zero_state = jnp.zeros((GDN_DK, GDN_DV), F32)
    lax.fori_loop(0, n_chunks, scan, (zero_state, zero_state))

    for t in range(n_tiles):
        sl = slice(t * rt, (t + 1) * rt)
        o = o_s[0, sl, :] + o_s[1, sl, :]
        o = o * lax.rsqrt(jnp.mean(o * o, axis=-1, keepdims=True) + RMS_EPS) * nw_ref[...]
        z = z_ref[sl, :]
        o_ref[sl, :] = (o * (z * _sigmoid(z))).astype(BF16)


def _gdn(g_qkvz, gates, conv_w, alog_lane, dtb_lane, norm_w, batch, seq):
    assert seq % GDN_PREP_ROWS == 0 and GDN_PREP_ROWS % GDN_CHUNK == 0
    n_chunks = seq // GDN_CHUNK
    assert n_chunks % (2 * GDN_PREP_PAIRS) == 0
    h = GDN_HEADS
    blk = (seq, LANES)
    cw_blk = (GDN_CONV, LANES)
    row_blk = (1, LANES)
    scratch_rows = 6 * seq + 2 * HALO + n_chunks * (2 * (2 * GDN_DK + GDN_CHUNK) + 2 * SUBLANES)
    vmem = 2 * (5 * seq * LANES * 4 + seq * LANES * 2) + scratch_rows * LANES * 4 + 8 * 1024 * 1024
    return pl.pallas_call(
        functools.partial(_gdn_kernel, seq=seq),
        grid=(batch, h),
        in_specs=[pl.BlockSpec(blk, lambda b, i: (b, i)),
                  pl.BlockSpec(blk, lambda b, i: (b, h + i)),
                  pl.BlockSpec(blk, lambda b, i: (b, 2 * h + i)),
                  pl.BlockSpec(blk, lambda b, i: (b, 3 * h + i)),
                  pl.BlockSpec(blk, lambda b, i: (b, 0)),
                  pl.BlockSpec(cw_blk, lambda b, i: (0, i)),
                  pl.BlockSpec(cw_blk, lambda b, i: (0, h + i)),
                  pl.BlockSpec(cw_blk, lambda b, i: (0, 2 * h + i)),
                  pl.BlockSpec(row_blk, lambda b, i: (0, 0)),
                  pl.BlockSpec(row_blk, lambda b, i: (0, 0)),
                  pl.BlockSpec(row_blk, lambda b, i: (0, 0))],
        out_specs=pl.BlockSpec(blk, lambda b, i: (b, i)),
        out_shape=jax.ShapeDtypeStruct((batch * seq, GDN_WIDTH), BF16),
        scratch_shapes=[pltpu.VMEM((seq + 2 * HALO, LANES), F32),
                        pltpu.VMEM((seq, LANES), F32), pltpu.VMEM((seq, LANES), F32),
                        pltpu.VMEM((seq, LANES), F32), pltpu.VMEM((seq, LANES), F32),
                        pltpu.VMEM((2, seq, GDN_DV), F32),
                        pltpu.VMEM((2, n_chunks, GDN_DK + GDN_CHUNK, GDN_DV), F32),
                        pltpu.VMEM((2, n_chunks, GDN_DK, GDN_DV), F32),
                        pltpu.VMEM((2, n_chunks, SUBLANES, LANES), F32)],
        compiler_params=pltpu.CompilerParams(dimension_semantics=("arbitrary", "arbitrary"),
                                             vmem_limit_bytes=_vmem_limit(vmem)),
        name="gdn",
    )(g_qkvz, g_qkvz, g_qkvz, g_qkvz, gates, conv_w, conv_w, conv_w, alog_lane, dtb_lane, norm_w)


OUT_PROJ_ROWS = 512


def _out_proj_kernel(na_ref, gd_ref, x_ref, w_ref, g_ref, b_ref, o_ref, *, alpha):
    mix = jnp.dot(na_ref[...], w_ref[0:NA_WIDTH, :], preferred_element_type=F32)
    mix = mix + jnp.dot(gd_ref[...], w_ref[NA_WIDTH:, :], preferred_element_type=F32)
    o_ref[...] = _layer_norm_rows(alpha * x_ref[...] + mix, g_ref[...], b_ref[...])


def _out_proj(na_out, gdn_out, x2d, w_out, ln_g, ln_b, alpha):
    t = x2d.shape[0]
    tm = OUT_PROJ_ROWS
    vmem = 2 * (2 * tm * NA_WIDTH * 2 + 2 * tm * D_MODEL * 4 + D_MODEL * D_MODEL * 2) + 4 * tm * D_MODEL * 4
    return pl.pallas_call(
        functools.partial(_out_proj_kernel, alpha=alpha),
        grid=(t // tm,),
        in_specs=[pl.BlockSpec((tm, NA_WIDTH), lambda i: (i, 0)),
                  pl.BlockSpec((tm, GDN_WIDTH), lambda i: (i, 0)),
                  pl.BlockSpec((tm, D_MODEL), lambda i: (i, 0)),
                  pl.BlockSpec((NA_WIDTH + GDN_WIDTH, D_MODEL), lambda i: (0, 0)),
                  pl.BlockSpec((1, D_MODEL), lambda i: (0, 0)),
                  pl.BlockSpec((1, D_MODEL), lambda i: (0, 0))],
        out_specs=pl.BlockSpec((tm, D_MODEL), lambda i: (i, 0)),
        out_shape=jax.ShapeDtypeStruct((t, D_MODEL), F32),
        compiler_params=pltpu.CompilerParams(dimension_semantics=("arbitrary",),
                                             vmem_limit_bytes=_vmem_limit(vmem)),
        name="out_proj",
    )(na_out, gdn_out, x2d, w_out, ln_g, ln_b)


FFN_SUB_ROWS = 1024
FFN_HALO = BF16_ROWS
FFN_LN_ROWS = 256


def _ffn_kernel(x_ref, wup_ref, cw_ref, cb_ref, wdn_ref, g_ref, b_ref, o_ref, xb_s, *, seq, alpha):
    c = pl.program_id(1)
    sub = min(FFN_SUB_ROWS, seq)

    @pl.when(c == 0)
    def _init():
        zero_halo = jnp.zeros((FFN_HALO, D_MODEL), BF16)
        xb_s[0:FFN_HALO, :] = zero_halo
        xb_s[FFN_HALO + seq:2 * FFN_HALO + seq, :] = zero_halo
        for t in range(seq // FFN_LN_ROWS):
            sl = slice(t * FFN_LN_ROWS, (t + 1) * FFN_LN_ROWS)
            x = x_ref[sl, :]
            xb_s[FFN_HALO + t * FFN_LN_ROWS:FFN_HALO + (t + 1) * FFN_LN_ROWS, :] = x.astype(BF16)
            o_ref[sl, :] = alpha * x

    cw = cw_ref[0]
    for rb in range(seq // sub):
        r0 = rb * sub
        xin = xb_s[r0:r0 + sub + 2 * FFN_HALO, :]
        h = jnp.dot(xin, wup_ref[0], preferred_element_type=F32)
        hc = cb_ref[0] + cw[0:1, :] * h[FFN_HALO - 1:FFN_HALO - 1 + sub, :]
        hc = hc + cw[1:2, :] * h[FFN_HALO:FFN_HALO + sub, :]
        hc = hc + cw[2:3, :] * h[FFN_HALO + 1:FFN_HALO + 1 + sub, :]
        gate = hc[:, :FF_CHUNK]
        act = (gate * _sigmoid(gate) * hc[:, FF_CHUNK:]).astype(BF16)
        o_ref[r0:r0 + sub, :] += jnp.dot(act, wdn_ref[0], preferred_element_type=F32)

    @pl.when(c == pl.num_programs(1) - 1)
    def _finish():
        for t in range(seq // FFN_LN_ROWS):
            sl = slice(t * FFN_LN_ROWS, (t + 1) * FFN_LN_ROWS)
            o_ref[sl, :] = _layer_norm_rows(o_ref[sl, :], g_ref[...], b_ref[...])


def _ffn(x1, wup_c, cw_c, cb_c, wdn_c, ln_g, ln_b, batch, seq, alpha):
    assert seq % FFN_LN_ROWS == 0 and seq % min(FFN_SUB_ROWS, seq) == 0
    sub = min(FFN_SUB_ROWS, seq)
    vmem = (4 * seq * D_MODEL * 4 + (seq + 2 * FFN_HALO) * D_MODEL * 2
            + 2 * (D_MODEL * 2 * FF_CHUNK * 2 + FF_CHUNK * D_MODEL * 2)
            + (sub + 2 * FFN_HALO) * 2 * FF_CHUNK * 4 * 3 + sub * D_MODEL * 4)
    return pl.pallas_call(
        functools.partial(_ffn_kernel, seq=seq, alpha=alpha),
        grid=(batch, N_FF_CHUNKS),
        in_specs=[pl.BlockSpec((seq, D_MODEL), lambda b, c: (b, 0)),
                  pl.BlockSpec((1, D_MODEL, 2 * FF_CHUNK), lambda b, c: (c, 0, 0)),
                  pl.BlockSpec((1, FFN_CONV, 2 * FF_CHUNK), lambda b, c: (c, 0, 0)),
                  pl.BlockSpec((1, 1, 2 * FF_CHUNK), lambda b, c: (c, 0, 0)),
                  pl.BlockSpec((1, FF_CHUNK, D_MODEL), lambda b, c: (c, 0, 0)),
                  pl.BlockSpec((1, D_MODEL), lambda b, c: (0, 0)),
                  pl.BlockSpec((1, D_MODEL), lambda b, c: (0, 0))],
        out_specs=pl.BlockSpec((seq, D_MODEL), lambda b, c: (b, 0)),
        out_shape=jax.ShapeDtypeStruct((batch * seq, D_MODEL), F32),
        scratch_shapes=[pltpu.VMEM((seq + 2 * FFN_HALO, D_MODEL), BF16)],
        compiler_params=pltpu.CompilerParams(dimension_semantics=("arbitrary", "arbitrary"),
                                             vmem_limit_bytes=_vmem_limit(vmem)),
        name="ffn",
    )(x1, wup_c, cw_c, cb_c, wdn_c, ln_g, ln_b)


def _chunk_ff(w, axis):
    w = jnp.moveaxis(w, axis, -1)
    lead = w.shape[:-1]
    w = w.reshape(lead + (2, N_FF_CHUNKS, FF_CHUNK))
    w = jnp.moveaxis(w, -2, 0)
    return w.reshape((N_FF_CHUNKS,) + lead + (2 * FF_CHUNK,))


def _prep_layer_params(w_in, na_rpb, gdn_conv_w, gdn_a_log, gdn_dt_bias, gdn_norm_w, w_out,
                       ln1_g, ln1_b, ffn_w_up, ffn_conv_w, ffn_conv_b, ffn_w_down, ln2_g, ln2_b):
    w_pad = jnp.pad(w_in, ((0, 0), (0, IN_COLS_PAD - IN_COLS))).astype(BF16)
    gate_pad = (2 * GDN_HEADS, LANES - N_GATES)
    alog_lane = jnp.pad(gdn_a_log.astype(F32).reshape(-1), gate_pad).reshape(1, LANES)
    dtb_lane = jnp.pad(gdn_dt_bias.astype(F32).reshape(-1), gate_pad).reshape(1, LANES)
    return dict(
        w_pad=w_pad,
        bias_tbl=_na_bias_table(na_rpb),
        conv_w=gdn_conv_w.astype(F32),
        alog_lane=alog_lane,
        dtb_lane=dtb_lane,
        norm_w=gdn_norm_w.astype(F32).reshape(1, GDN_DV),
        w_out=w_out.astype(BF16),
        ln1_g=ln1_g.astype(F32).reshape(1, D_MODEL), ln1_b=ln1_b.astype(F32).reshape(1, D_MODEL),
        wup_c=_chunk_ff(ffn_w_up, 1).astype(BF16),
        cw_c=_chunk_ff(ffn_conv_w.astype(F32), 1),
        cb_c=_chunk_ff(ffn_conv_b.astype(F32).reshape(1, -1), 1),
        wdn_c=ffn_w_down.astype(BF16).reshape(N_FF_CHUNKS, FF_CHUNK, D_MODEL),
        ln2_g=ln2_g.astype(F32).reshape(1, D_MODEL), ln2_b=ln2_b.astype(F32).reshape(1, D_MODEL),
    )


def _layer(x2d, p, batch, seq, alpha):
    na_qkv, g_qkvz, gates = _in_proj(x2d, p["w_pad"])
    na_out = _na_attention(na_qkv, p["bias_tbl"], batch, seq)
    gdn_out = _gdn(g_qkvz, gates, p["conv_w"], p["alog_lane"], p["dtb_lane"], p["norm_w"], batch, seq)
    x1 = _out_proj(na_out, gdn_out, x2d, p["w_out"], p["ln1_g"], p["ln1_b"], alpha)
    return _ffn(x1, p["wup_c"], p["cw_c"], p["cb_c"], p["wdn_c"], p["ln2_g"], p["ln2_b"], batch, seq, alpha)


def _trunk(x, layer_params, alpha):
    batch, seq, d = x.shape
    x2d = x.reshape(batch * seq, d)
    for p in layer_params:
        x2d = _layer(x2d, p, batch, seq, alpha)
    return x2d.reshape(batch, seq, d)


def kernel(x_prompt, x_sample, w_in, na_rpb, gdn_conv_w, gdn_a_log, gdn_dt_bias, gdn_norm_w, w_out, ln1_g, ln1_b,
           ffn_w_up, ffn_conv_w, ffn_conv_b, ffn_w_down, ln2_g, ln2_b):
    depth = w_in.shape[0]
    alpha = float((2 * depth) ** 0.25)
    stacked = (w_in, na_rpb, gdn_conv_w, gdn_a_log, gdn_dt_bias, gdn_norm_w, w_out, ln1_g, ln1_b,
               ffn_w_up, ffn_conv_w, ffn_conv_b, ffn_w_down, ln2_g, ln2_b)
    layer_params = [_prep_layer_params(*(a[l] for a in stacked)) for l in range(depth)]
    return (_trunk(x_prompt, layer_params, alpha), _trunk(x_sample, layer_params, alpha))
```

```python
import functools

import jax
import jax.numpy as jnp
import numpy as np
from jax import lax
from jax.experimental import pallas as pl
from jax.experimental.pallas import tpu as pltpu

F32 = jnp.float32
BF16 = jnp.bfloat16

D_MODEL = 1024
GRID_W = 64
NA_HEADS = 8
NA_HEAD_DIM = 64
NA_WIN_ROWS = 8
NA_WIN_COLS = 16
GDN_HEADS = 4
GDN_DK = 128
GDN_DV = 128
GDN_CONV = 5
GDN_CHUNK = 64
D_FF = 2816
FFN_CONV = 3
LN_EPS = 1e-5
RMS_EPS = 1e-6

NA_WIDTH = NA_HEADS * NA_HEAD_DIM
GDN_WIDTH = GDN_HEADS * GDN_DK
NA_COLS = 3 * NA_WIDTH
GDN_COLS = 4 * GDN_WIDTH
N_GATES = 4 * GDN_HEADS
IN_COLS = NA_COLS + GDN_COLS + N_GATES

LANES = 128
SUBLANES = 8
BF16_ROWS = 16
VMEM_LIMIT_CAP = 56 * 1024 * 1024

IN_COLS_PAD = NA_COLS + GDN_COLS + LANES
FF_CHUNK = 256
N_FF_CHUNKS = D_FF // FF_CHUNK
NA_SCALE = NA_HEAD_DIM ** -0.5
GDN_Q_SCALE = GDN_DK ** -0.5

LANE_BETA_F, LANE_BETA_B, LANE_G_F, LANE_G_B = 0, GDN_HEADS, 2 * GDN_HEADS, 3 * GDN_HEADS


def _vmem_limit(nbytes):
    return int(min(VMEM_LIMIT_CAP, max(16 * 1024 * 1024, nbytes)))


def _sigmoid(x):
    return 1.0 / (1.0 + jnp.exp(-x))


def _layer_norm_rows(y, g, b):
    mu = jnp.mean(y, axis=-1, keepdims=True)
    yc = y - mu
    var = jnp.mean(yc * yc, axis=-1, keepdims=True)
    return yc * lax.rsqrt(var + LN_EPS) * g + b


IN_PROJ_ROWS = 512
IN_PROJ_COL_CHUNK = 512


IN_PROJ_HALO = BF16_ROWS
IN_PROJ_EPI_ROWS = 64
GDN_CONV_COLS = 3 * GDN_WIDTH


def _gate_tile(x, alog, dtb):
    rows = x.shape[0]
    c = GDN_CHUNK
    lane = lax.broadcasted_iota(jnp.int32, (rows, LANES), 1)
    ri = lax.broadcasted_iota(jnp.int32, (rows, LANES), 0) % c
    a = x + dtb
    softplus = jnp.maximum(a, 0.0) + jnp.log(1.0 + jnp.exp(-jnp.abs(a)))
    g = -jnp.exp(alog) * softplus
    pre = g
    suf = g
    s = 1
    while s < c:
        pre = pre + jnp.where(ri >= s, pltpu.roll(pre, s, axis=0), 0.0)
        suf = suf + jnp.where(ri < c - s, pltpu.roll(suf, rows - s, axis=0), 0.0)
        s *= 2
    is_fwd = (lane >= LANE_G_F) & (lane < LANE_G_B)
    is_bwd = (lane >= LANE_G_B) & (lane < LANE_G_B + GDN_HEADS)
    return jnp.where(is_fwd, pre, jnp.where(is_bwd, suf, _sigmoid(x)))


def _in_proj_kernel(x_ref, xprev_ref, xnext_ref, w_ref, cw_ref, alog_ref, dtb_ref, na_ref, g_ref, gate_ref,
                    xe_s, hp_s, *, tiles_per_seq):
    tm = IN_PROJ_ROWS
    halo = IN_PROJ_HALO
    cc = IN_PROJ_COL_CHUNK
    pos = pl.program_id(0) % tiles_per_seq
    xe_s[0:halo, :] = xprev_ref[...].astype(BF16)
    xe_s[halo:halo + tm, :] = x_ref[...].astype(BF16)
    xe_s[halo + tm:, :] = xnext_ref[...].astype(BF16)

    xe = xe_s[...]
    n_blk = GDN_CONV_COLS // LANES
    for c in range(GDN_CONV_COLS // cc):
        lo = c * cc
        hp = jnp.dot(xe, w_ref[:, NA_COLS + lo:NA_COLS + lo + cc], preferred_element_type=F32)
        for b in range(cc // LANES):
            hp_s[lo // LANES + b] = hp[:, b * LANES:(b + 1) * LANES]

    @pl.when(pos == 0)
    def _zero_prev():
        hp_s[:, 0:halo, :] = jnp.zeros((n_blk, halo, LANES), F32)

    @pl.when(pos == tiles_per_seq - 1)
    def _zero_next():
        hp_s[:, halo + tm:, :] = jnp.zeros((n_blk, halo, LANES), F32)

    er = IN_PROJ_EPI_ROWS
    pad = GDN_CONV // 2

    def conv_unit(u):
        blk, t = divmod(u, tm // er)
        cols = slice(blk * LANES, (blk + 1) * LANES)
        base = halo + t * er - pad
        y = hp_s[blk, base:base + er, :] * cw_ref[0:1, cols]
        for j in range(1, GDN_CONV):
            y = y + hp_s[blk, base + j:base + j + er, :] * cw_ref[j:j + 1, cols]
        y = y * _sigmoid(y)
        if blk < 2 * GDN_HEADS:
            y = y * lax.rsqrt(jnp.sum(y * y, axis=-1, keepdims=True) + RMS_EPS)
            if blk < GDN_HEADS:
                y = y * GDN_Q_SCALE
        g_ref[t * er:(t + 1) * er, cols] = y

    n_units = (GDN_CONV_COLS // LANES) * (tm // er)
    n_dots = NA_COLS // cc + 2
    units_per_dot = -(-n_units // n_dots)
    xb = xe_s[halo:halo + tm, :]
    for c in range(n_dots):
        if c < NA_COLS // cc:
            lo = c * cc
            acc = jnp.dot(xb, w_ref[:, lo:lo + cc], preferred_element_type=F32)
            if lo < NA_WIDTH:
                acc = acc * NA_SCALE
            na_ref[:, lo:lo + cc] = acc.astype(BF16)
        elif c == NA_COLS // cc:
            g_ref[:, GDN_CONV_COLS:] = jnp.dot(xb, w_ref[:, NA_COLS + GDN_CONV_COLS:NA_COLS + GDN_COLS],
                                               preferred_element_type=F32)
        else:
            gates = jnp.dot(xb, w_ref[:, NA_COLS + GDN_COLS:], preferred_element_type=F32)
            for t in range(tm // er):
                gate_ref[t * er:(t + 1) * er, :] = _gate_tile(gates[t * er:(t + 1) * er, :],
                                                               alog_ref[...], dtb_ref[...])
        for u in range(c * units_per_dot, min((c + 1) * units_per_dot, n_units)):
            conv_unit(u)


def _in_proj(x2d, w_pad, conv_w, alog_lane, dtb_lane, seq):
    t = x2d.shape[0]
    tm = IN_PROJ_ROWS
    halo = IN_PROJ_HALO
    assert seq % tm == 0 and tm % IN_PROJ_EPI_ROWS == 0 and IN_PROJ_EPI_ROWS % GDN_CHUNK == 0
    halo_blocks = tm // halo
    last_halo_block = t // halo - 1
    row_blk = (1, LANES)
    vmem = (2 * (tm * D_MODEL * 4 + D_MODEL * IN_COLS_PAD * 2 + tm * NA_COLS * 2 + tm * GDN_COLS * 4
                 + tm * LANES * 4 + 2 * halo * D_MODEL * 4)
            + (tm + 2 * halo) * (D_MODEL * 2 + GDN_CONV_COLS * 4) + 6 * (tm + 2 * halo) * IN_PROJ_COL_CHUNK * 4)
    return pl.pallas_call(
        functools.partial(_in_proj_kernel, tiles_per_seq=seq // tm),
        grid=(t // tm,),
        in_specs=[pl.BlockSpec((tm, D_MODEL), lambda i: (i, 0)),
                  pl.BlockSpec((halo, D_MODEL), lambda i: (jnp.maximum(i * halo_blocks - 1, 0), 0)),
                  pl.BlockSpec((halo, D_MODEL), lambda i: (jnp.minimum((i + 1) * halo_blocks, last_halo_block), 0)),
                  pl.BlockSpec((D_MODEL, IN_COLS_PAD), lambda i: (0, 0)),
                  pl.BlockSpec((GDN_CONV, GDN_CONV_COLS), lambda i: (0, 0)),
                  pl.BlockSpec(row_blk, lambda i: (0, 0)),
                  pl.BlockSpec(row_blk, lambda i: (0, 0))],
        out_specs=[pl.BlockSpec((tm, NA_COLS), lambda i: (i, 0)),
                   pl.BlockSpec((tm, GDN_COLS), lambda i: (i, 0)),
                   pl.BlockSpec((tm, LANES), lambda i: (i, 0))],
        out_shape=[jax.ShapeDtypeStruct((t, NA_COLS), BF16),
                   jax.ShapeDtypeStruct((t, GDN_COLS), F32),
                   jax.ShapeDtypeStruct((t, LANES), F32)],
        scratch_shapes=[pltpu.VMEM((tm + 2 * halo, D_MODEL), BF16),
                        pltpu.VMEM((GDN_CONV_COLS // LANES, tm + 2 * halo, LANES), F32)],
        compiler_params=pltpu.CompilerParams(dimension_semantics=("arbitrary",),
                                             vmem_limit_bytes=_vmem_limit(vmem)),
        name="in_proj",
    )(x2d, x2d, x2d, w_pad, conv_w, alog_lane, dtb_lane)


NA_KEYS = NA_WIN_ROWS * GRID_W
NA_ROWS_PER_STEP = 8


def _na_kernel(q_ref, k_ref, v_ref, bias_ref, o_ref, *, rows):
    lane = lax.broadcasted_iota(jnp.int32, (GRID_W, LANES), 1)
    first = lane < NA_HEAD_DIM
    win = min(NA_WIN_ROWS, rows)

    def row_group(g, carry):
        chains = []
        for j in range(NA_ROWS_PER_STEP):
            r = g * NA_ROWS_PER_STEP + j
            rs = jnp.clip(r - win // 2, 0, rows - win)
            dr0 = rs - r + (NA_WIN_ROWS - 1)
            q = q_ref[pl.ds(pl.multiple_of(r * GRID_W, GRID_W), GRID_W), :]
            k = k_ref[pl.ds(pl.multiple_of(rs * GRID_W, GRID_W), NA_KEYS), :]
            v = v_ref[pl.ds(pl.multiple_of(rs * GRID_W, GRID_W), NA_KEYS), :]
            for hh in range(2):
                keep = first if hh == 0 else jnp.logical_not(first)
                chains.append(dict(r=r, hh=hh, dr0=dr0, k=k, v=v, qm=jnp.where(keep, q, jnp.zeros_like(q))))
        for ch in chains:
            s = lax.dot_general(ch["qm"], ch["k"], (((1,), (1,)), ((), ())), preferred_element_type=F32)
            ch["s"] = s + bias_ref[ch["hh"], ch["dr0"]]
        for ch in chains:
            ch["m"] = jnp.max(ch["s"], axis=-1, keepdims=True)
        for ch in chains:
            e = jnp.exp(ch["s"] - ch["m"])
            ch["inv"] = 1.0 / jnp.sum(e, axis=-1, keepdims=True)
            ch["e"] = e.astype(BF16)
        for ch in chains:
            ch["o"] = jnp.dot(ch["e"], ch["v"], preferred_element_type=F32) * ch["inv"]
        for j in range(NA_ROWS_PER_STEP):
            o = jnp.where(first, chains[2 * j]["o"], chains[2 * j + 1]["o"])
            r = chains[2 * j]["r"]
            o_ref[pl.ds(pl.multiple_of(r * GRID_W, GRID_W), GRID_W), :] = o.astype(BF16)
        return carry

    lax.fori_loop(0, rows // NA_ROWS_PER_STEP, row_group, 0)


def _na_attention(na_qkv, bias_tbl, batch, seq):
    rows = seq // GRID_W
    assert rows >= NA_WIN_ROWS and rows % NA_ROWS_PER_STEP == 0
    pairs = NA_HEADS // 2
    blk = (seq, LANES)
    vmem = 2 * (4 * seq * LANES * 2 + 2 * NA_WIN_ROWS * GRID_W * NA_KEYS * 4) + 16 * GRID_W * NA_KEYS * 4
    return pl.pallas_call(
        functools.partial(_na_kernel, rows=rows),
        grid=(batch, pairs),
        in_specs=[pl.BlockSpec(blk, lambda b, p: (b, p)),
                  pl.BlockSpec(blk, lambda b, p: (b, pairs + p)),
                  pl.BlockSpec(blk, lambda b, p: (b, 2 * pairs + p)),
                  pl.BlockSpec((2, NA_WIN_ROWS, GRID_W, NA_KEYS), lambda b, p: (p, 0, 0, 0))],
        out_specs=pl.BlockSpec(blk, lambda b, p: (b, p)),
        out_shape=jax.ShapeDtypeStruct((batch * seq, NA_WIDTH), BF16),
        compiler_params=pltpu.CompilerParams(dimension_semantics=("arbitrary", "arbitrary"),
                                             vmem_limit_bytes=_vmem_limit(vmem)),
        name="na_attn",
    )(na_qkv, na_qkv, na_qkv, bias_tbl)


def _na_bias_table(rpb):
    c = np.arange(GRID_W)[:, None]
    kc = np.arange(GRID_W)[None, :]
    start = np.clip(c - NA_WIN_COLS // 2, 0, GRID_W - NA_WIN_COLS)
    mask = (kc >= start) & (kc < start + NA_WIN_COLS)
    dc = np.clip(kc - c + NA_WIN_COLS - 1, 0, 2 * NA_WIN_COLS - 2)
    colb = jnp.where(jnp.asarray(mask), rpb[:, :, dc].astype(F32), -1e30)
    dr = np.arange(NA_WIN_ROWS)[:, None] + np.arange(NA_WIN_ROWS)[None, :]
    tbl = colb[:, dr]
    return tbl.transpose(0, 1, 3, 2, 4).reshape(NA_HEADS, NA_WIN_ROWS, GRID_W, NA_KEYS)


GDN_PREP_ROWS = 256
GDN_PREP_PAIRS = 8


def _lane_pair(a, b):
    return jnp.concatenate([a, b], axis=1)


def _block_diag_rows(y, left):
    zero = jnp.zeros_like(y)
    return jnp.concatenate([jnp.where(left, y, zero), jnp.where(left, zero, y)], axis=0)


def _gdn_prepare_group(base, q_s, k_s, v_s, gt_s, mq_s, c_s, o_s, cd_s):
    c = GDN_CHUNK
    lane = lax.broadcasted_iota(jnp.int32, (c, 2 * c), 1)
    ri = lax.broadcasted_iota(jnp.int32, (c, 2 * c), 0)
    left = lane < c
    ci = jnp.where(left, lane, lane - c)
    top = lax.broadcasted_iota(jnp.int32, (2 * c, 1), 0) < c
    incl = [ci <= ri, ci >= ri]
    strict = [ci < ri, ci > ri]
    lane_beta = (LANE_BETA_F, LANE_BETA_B)
    lane_g = (LANE_G_F, LANE_G_B)
    zero_tile = jnp.zeros((c, LANES), F32)
    zero_wide = jnp.zeros((c, GDN_DK + GDN_DV), F32)

    chains = []
    for p in range(GDN_PREP_PAIRS):
        n0 = base + 2 * p
        r0 = pl.multiple_of(n0 * c, 2 * c)
        q2 = q_s[pl.ds(r0, 2 * c), :]
        k2 = k_s[pl.ds(r0, 2 * c), :]
        v2 = v_s[pl.ds(r0, 2 * c), :]
        gt2 = gt_s[pl.ds(r0, 2 * c), :]
        gt2_t = gt2.T
        kb2 = [k2 * gt2[:, lane_beta[d]:lane_beta[d] + 1] for d in range(2)]
        lhs = jnp.concatenate([_lane_pair(kb2[0][:c], kb2[0][c:]), _lane_pair(kb2[1][:c], kb2[1][c:]),
                               _lane_pair(q2[:c], q2[c:])], axis=0)
        k_diag = jnp.concatenate([_lane_pair(k2[:c], zero_tile), _lane_pair(zero_tile, k2[c:])], axis=0)
        kq = lax.dot_general(lhs, k_diag, (((1,), (1,)), ((), ())), preferred_element_type=F32)
        for d in range(2):
            gc2 = gt2[:, lane_g[d]:lane_g[d] + 1]
            gcol = jnp.where(left, gc2[:c], gc2[c:])
            grow = gt2_t[lane_g[d]:lane_g[d] + 1, :]
            decay = jnp.where(incl[d], jnp.exp(jnp.where(incl[d], gcol - grow, 0.0)), 0.0)
            e0 = (0 if d else c - 1)
            g_end = [gc2[e0:e0 + 1], gc2[c + e0:c + e0 + 1]]
            ge2 = jnp.where(top, g_end[0], g_end[1])
            eg2 = jnp.exp(gc2)
            kbe2 = kb2[d] * eg2
            vb2 = v2 * gt2[:, lane_beta[d]:lane_beta[d] + 1]
            chains.append(dict(
                d=d, n0=n0, r0=r0, g_end=g_end,
                neg_l=-jnp.where(strict[d], kq[d * c:(d + 1) * c] * decay, 0.0),
                attn=kq[2 * c:] * decay,
                rhs=[_lane_pair(kbe2[:c], vb2[:c]), _lane_pair(kbe2[c:], vb2[c:])],
                qe2=q2 * eg2,
                kd2=k2 * jnp.exp(ge2 - gc2)))

    for ch in chains:
        ch["p"] = ch["neg_l"]
        ch["y"] = jnp.dot(ch["neg_l"], _block_diag_rows(ch["neg_l"], left), preferred_element_type=F32)
    steps = int(np.log2(c)) - 1
    for i in range(steps):
        last = i == steps - 1
        for ch in chains:
            lhs = ch["p"] if last else jnp.concatenate([ch["p"], ch["y"]], axis=0)
            prod = jnp.dot(lhs, _block_diag_rows(ch["y"], left), preferred_element_type=F32)
            ch["p"] = ch["p"] + ch["y"] + prod[:c]
            if not last:
                ch["y"] = prod[c:]

    for ch in chains:
        ch["sol"] = [ch["rhs"][0] + jnp.dot(ch["p"], jnp.concatenate([ch["rhs"][0], zero_wide], axis=0),
                                            preferred_element_type=F32),
                     ch["rhs"][1] + jnp.dot(ch["p"], jnp.concatenate([zero_wide, ch["rhs"][1]], axis=0),
                                            preferred_element_type=F32)]
    for ch in chains:
        lhs = jnp.concatenate([ch["kd2"].T, ch["attn"]], axis=0)
        ch["big"] = [jnp.dot(lhs, jnp.concatenate([ch["sol"][0], zero_wide], axis=0), preferred_element_type=F32),
                     jnp.dot(lhs, jnp.concatenate([zero_wide, ch["sol"][1]], axis=0), preferred_element_type=F32)]
    for ch in chains:
        ops = []
        for j in range(2):
            big = ch["big"][j]
            ops.append(dict(m=-big[:GDN_DK, :GDN_DK], c=big[:GDN_DK, GDN_DK:],
                            q=ch["qe2"][j * c:(j + 1) * c] - big[GDN_DK:, :GDN_DK], o=big[GDN_DK:, GDN_DK:],
                            cd=jnp.exp(ch["g_end"][j])))
        ch["fst"], ch["sec"] = (ops[1], ops[0]) if ch["d"] else (ops[0], ops[1])
    for ch in chains:
        fst, sec = ch["fst"], ch["sec"]
        ch["comp"] = jnp.dot(jnp.concatenate([sec["m"], sec["q"]], axis=0), _lane_pair(fst["m"], fst["c"]),
                             preferred_element_type=F32)
    for ch in chains:
        d, fst, sec, comp = ch["d"], ch["fst"], ch["sec"], ch["comp"]
        pair = ch["n0"] // 2
        m_pair = sec["cd"] * fst["m"] + fst["cd"] * sec["m"] + comp[:GDN_DK, :GDN_DK]
        q_sec = fst["cd"] * sec["q"] + comp[GDN_DK:, :GDN_DK]
        q_rows = [q_sec, fst["q"]] if d else [fst["q"], q_sec]
        o_sec = sec["o"] + comp[GDN_DK:, GDN_DK:]
        o_rows = [o_sec, fst["o"]] if d else [fst["o"], o_sec]
        mq_s[d, pair] = jnp.concatenate([m_pair] + q_rows, axis=0)
        c_s[d, pair] = sec["cd"] * fst["c"] + comp[:GDN_DK, GDN_DK:] + sec["c"]
        o_s[d, pl.ds(ch["r0"], 2 * c), :] = jnp.concatenate(o_rows, axis=0)
        cd_s[d, pair] = jnp.broadcast_to(fst["cd"] * sec["cd"], (SUBLANES, LANES))


def _gdn_kernel(q_ref, k_ref, v_ref, z_ref, gate_ref, nw_ref, o_ref, gt_s, o_s, mq_s, c_s, cd_s, *, seq):
    head = pl.program_id(1)
    c = GDN_CHUNK
    n_chunks = seq // c
    rt = GDN_PREP_ROWS
    n_tiles = seq // rt

    shift = (LANES - head) % LANES
    for t in range(n_tiles):
        gt_s[t * rt:(t + 1) * rt, :] = pltpu.roll(gate_ref[t * rt:(t + 1) * rt, :], shift, axis=1)

    def prepare(i, carry):
        _gdn_prepare_group(i * (2 * GDN_PREP_PAIRS), q_ref, k_ref, v_ref, gt_s, mq_s, c_s, o_s, cd_s)
        return carry

    lax.fori_loop(0, n_chunks // (2 * GDN_PREP_PAIRS), prepare, 0)

    n_pairs = n_chunks // 2

    def scan(i, states):
        new_states = []
        for d, p in ((0, i), (1, n_pairs - 1 - i)):
            s = states[d]
            z = jnp.dot(mq_s[d, p], s, preferred_element_type=F32)
            rows = pl.ds(pl.multiple_of(p * (2 * c), 2 * c), 2 * c)
            o_s[d, rows, :] = o_s[d, rows, :] + z[GDN_DK:]
            new_states.append(s * cd_s[d, p][0:1, :] + z[:GDN_DK] + c_s[d, p])
        return tuple(new_states)

    zero_state = jnp.zeros((GDN_DK, GDN_DV), F32)
    lax.fori_loop(0, n_pairs, scan, (zero_state, zero_state))

    for t in range(n_tiles):
        sl = slice(t * rt, (t + 1) * rt)
        o = o_s[0, sl, :] + o_s[1, sl, :]
        o = o * lax.rsqrt(jnp.mean(o * o, axis=-1, keepdims=True) + RMS_EPS) * nw_ref[...]
        z = z_ref[sl, :]
        o_ref[sl, :] = (o * (z * _sigmoid(z))).astype(BF16)


def _gdn(g_qkvz, gates, norm_w, batch, seq):
    assert seq % GDN_PREP_ROWS == 0
    n_chunks = seq // GDN_CHUNK
    assert n_chunks % (2 * GDN_PREP_PAIRS) == 0
    h = GDN_HEADS
    blk = (seq, LANES)
    n_pairs = n_chunks // 2
    scratch_rows = 3 * seq + n_pairs * (2 * (2 * GDN_DK + 2 * GDN_CHUNK) + 2 * SUBLANES)
    vmem = 2 * (5 * seq * LANES * 4 + seq * LANES * 2) + scratch_rows * LANES * 4 + 8 * 1024 * 1024
    return pl.pallas_call(
        functools.partial(_gdn_kernel, seq=seq),
        grid=(batch, h),
        in_specs=[pl.BlockSpec(blk, lambda b, i: (b, i)),
                  pl.BlockSpec(blk, lambda b, i: (b, h + i)),
                  pl.BlockSpec(blk, lambda b, i: (b, 2 * h + i)),
                  pl.BlockSpec(blk, lambda b, i: (b, 3 * h + i)),
                  pl.BlockSpec(blk, lambda b, i: (b, 0)),
                  pl.BlockSpec((1, LANES), lambda b, i: (0, 0))],
        out_specs=pl.BlockSpec(blk, lambda b, i: (b, i)),
        out_shape=jax.ShapeDtypeStruct((batch * seq, GDN_WIDTH), BF16),
        scratch_shapes=[pltpu.VMEM((seq, LANES), F32),
                        pltpu.VMEM((2, seq, GDN_DV), F32),
                        pltpu.VMEM((2, n_pairs, GDN_DK + 2 * GDN_CHUNK, GDN_DV), F32),
                        pltpu.VMEM((2, n_pairs, GDN_DK, GDN_DV), F32),
                        pltpu.VMEM((2, n_pairs, SUBLANES, LANES), F32)],
        compiler_params=pltpu.CompilerParams(dimension_semantics=("arbitrary", "arbitrary"),
                                             vmem_limit_bytes=_vmem_limit(vmem)),
        name="gdn",
    )(g_qkvz, g_qkvz, g_qkvz, g_qkvz, gates, norm_w)


OUT_PROJ_ROWS = 512


def _out_proj_kernel(na_ref, gd_ref, x_ref, w_ref, g_ref, b_ref, o_ref, *, alpha):
    mix = jnp.dot(na_ref[...], w_ref[0:NA_WIDTH, :], preferred_element_type=F32)
    mix = mix + jnp.dot(gd_ref[...], w_ref[NA_WIDTH:, :], preferred_element_type=F32)
    o_ref[...] = _layer_norm_rows(alpha * x_ref[...] + mix, g_ref[...], b_ref[...])


def _out_proj(na_out, gdn_out, x2d, w_out, ln_g, ln_b, alpha):
    t = x2d.shape[0]
    tm = OUT_PROJ_ROWS
    vmem = 2 * (2 * tm * NA_WIDTH * 2 + 2 * tm * D_MODEL * 4 + D_MODEL * D_MODEL * 2) + 4 * tm * D_MODEL * 4
    return pl.pallas_call(
        functools.partial(_out_proj_kernel, alpha=alpha),
        grid=(t // tm,),
        in_specs=[pl.BlockSpec((tm, NA_WIDTH), lambda i: (i, 0)),
                  pl.BlockSpec((tm, GDN_WIDTH), lambda i: (i, 0)),
                  pl.BlockSpec((tm, D_MODEL), lambda i: (i, 0)),
                  pl.BlockSpec((NA_WIDTH + GDN_WIDTH, D_MODEL), lambda i: (0, 0)),
                  pl.BlockSpec((1, D_MODEL), lambda i: (0, 0)),
                  pl.BlockSpec((1, D_MODEL), lambda i: (0, 0))],
        out_specs=pl.BlockSpec((tm, D_MODEL), lambda i: (i, 0)),
        out_shape=jax.ShapeDtypeStruct((t, D_MODEL), F32),
        compiler_params=pltpu.CompilerParams(dimension_semantics=("arbitrary",),
                                             vmem_limit_bytes=_vmem_limit(vmem)),
        name="out_proj",
    )(na_out, gdn_out, x2d, w_out, ln_g, ln_b)


FFN_SUB_ROWS = 1024
FFN_HALO = BF16_ROWS
FFN_LN_ROWS = 256
FFN_ACT_ROWS = 32


def _ffn_kernel(x_ref, wup_ref, cw_ref, cb_ref, wdn_ref, g_ref, b_ref, o_ref, xb_s, h_s, act_s, *, seq, alpha):
    c = pl.program_id(1)
    sub = min(FFN_SUB_ROWS, seq)
    rt = FFN_ACT_ROWS

    @pl.when(c == 0)
    def _init():
        zero_halo = jnp.zeros((FFN_HALO, D_MODEL), BF16)
        xb_s[0:FFN_HALO, :] = zero_halo
        xb_s[FFN_HALO + seq:2 * FFN_HALO + seq, :] = zero_halo
        for t in range(seq // FFN_LN_ROWS):
            sl = slice(t * FFN_LN_ROWS, (t + 1) * FFN_LN_ROWS)
            x = x_ref[sl, :]
            xb_s[FFN_HALO + t * FFN_LN_ROWS:FFN_HALO + (t + 1) * FFN_LN_ROWS, :] = x.astype(BF16)
            o_ref[sl, :] = alpha * x

    cw = cw_ref[0]
    n_sub = seq // sub
    n_blk = 2 * FF_CHUNK // LANES
    for rb in range(n_sub):
        xin = xb_s[rb * sub:rb * sub + sub + 2 * FFN_HALO, :]
        h = jnp.dot(xin, wup_ref[0], preferred_element_type=F32)
        for b in range(n_blk):
            h_s[rb, b] = h[:, b * LANES:(b + 1) * LANES]
    for rb in range(n_sub):
        for t in range(sub // rt):
            base = FFN_HALO + t * rt - FFN_CONV // 2
            hcs = []
            for b in range(n_blk):
                cols = slice(b * LANES, (b + 1) * LANES)
                hc = cb_ref[0, :, cols] + cw[0:1, cols] * h_s[rb, b, base:base + rt, :]
                for j in range(1, FFN_CONV):
                    hc = hc + cw[j:j + 1, cols] * h_s[rb, b, base + j:base + j + rt, :]
                hcs.append(hc)
            gate = jnp.concatenate(hcs[:n_blk // 2], axis=1)
            val = jnp.concatenate(hcs[n_blk // 2:], axis=1)
            act_s[rb, t * rt:(t + 1) * rt, :] = (gate * _sigmoid(gate) * val).astype(BF16)
        o_ref[rb * sub:(rb + 1) * sub, :] += jnp.dot(act_s[rb], wdn_ref[0], preferred_element_type=F32)

    @pl.when(c == pl.num_programs(1) - 1)
    def _finish():
        for t in range(seq // FFN_LN_ROWS):
            sl = slice(t * FFN_LN_ROWS, (t + 1) * FFN_LN_ROWS)
            o_ref[sl, :] = _layer_norm_rows(o_ref[sl, :], g_ref[...], b_ref[...])


def _ffn(x1, wup_c, cw_c, cb_c, wdn_c, ln_g, ln_b, batch, seq, alpha):
    assert seq % FFN_LN_ROWS == 0 and seq % min(FFN_SUB_ROWS, seq) == 0
    sub = min(FFN_SUB_ROWS, seq)
    assert sub % FFN_ACT_ROWS == 0
    vmem = (4 * seq * D_MODEL * 4 + (seq + 2 * FFN_HALO) * D_MODEL * 2
            + 2 * (D_MODEL * 2 * FF_CHUNK * 2 + FF_CHUNK * D_MODEL * 2)
            + (sub + 2 * FFN_HALO) * 2 * FF_CHUNK * 4 * 3 + sub * D_MODEL * 4)
    return pl.pallas_call(
        functools.partial(_ffn_kernel, seq=seq, alpha=alpha),
        grid=(batch, N_FF_CHUNKS),
        in_specs=[pl.BlockSpec((seq, D_MODEL), lambda b, c: (b, 0)),
                  pl.BlockSpec((1, D_MODEL, 2 * FF_CHUNK), lambda b, c: (c, 0, 0)),
                  pl.BlockSpec((1, FFN_CONV, 2 * FF_CHUNK), lambda b, c: (c, 0, 0)),
                  pl.BlockSpec((1, 1, 2 * FF_CHUNK), lambda b, c: (c, 0, 0)),
                  pl.BlockSpec((1, FF_CHUNK, D_MODEL), lambda b, c: (c, 0, 0)),
                  pl.BlockSpec((1, D_MODEL), lambda b, c: (0, 0)),
                  pl.BlockSpec((1, D_MODEL), lambda b, c: (0, 0))],
        out_specs=pl.BlockSpec((seq, D_MODEL), lambda b, c: (b, 0)),
        out_shape=jax.ShapeDtypeStruct((batch * seq, D_MODEL), F32),
        scratch_shapes=[pltpu.VMEM((seq + 2 * FFN_HALO, D_MODEL), BF16),
                        pltpu.VMEM((seq // sub, 2 * FF_CHUNK // LANES, sub + 2 * FFN_HALO, LANES), F32),
                        pltpu.VMEM((seq // sub, sub, FF_CHUNK), BF16)],
        compiler_params=pltpu.CompilerParams(dimension_semantics=("arbitrary", "arbitrary"),
                                             vmem_limit_bytes=_vmem_limit(vmem)),
        name="ffn",
    )(x1, wup_c, cw_c, cb_c, wdn_c, ln_g, ln_b)


def _chunk_ff(w, axis):
    w = jnp.moveaxis(w, axis, -1)
    lead = w.shape[:-1]
    w = w.reshape(lead + (2, N_FF_CHUNKS, FF_CHUNK))
    w = jnp.moveaxis(w, -2, 0)
    return w.reshape((N_FF_CHUNKS,) + lead + (2 * FF_CHUNK,))


def _prep_layer_params(w_in, na_rpb, gdn_conv_w, gdn_a_log, gdn_dt_bias, gdn_norm_w, w_out,
                       ln1_g, ln1_b, ffn_w_up, ffn_conv_w, ffn_conv_b, ffn_w_down, ln2_g, ln2_b):
    w_pad = jnp.pad(w_in, ((0, 0), (0, IN_COLS_PAD - IN_COLS))).astype(BF16)
    gate_pad = (2 * GDN_HEADS, LANES - N_GATES)
    alog_lane = jnp.pad(gdn_a_log.astype(F32).reshape(-1), gate_pad).reshape(1, LANES)
    dtb_lane = jnp.pad(gdn_dt_bias.astype(F32).reshape(-1), gate_pad).reshape(1, LANES)
    return dict(
        w_pad=w_pad,
        bias_tbl=_na_bias_table(na_rpb),
        conv_w=gdn_conv_w.astype(F32),
        alog_lane=alog_lane,
        dtb_lane=dtb_lane,
        norm_w=gdn_norm_w.astype(F32).reshape(1, GDN_DV),
        w_out=w_out.astype(BF16),
        ln1_g=ln1_g.astype(F32).reshape(1, D_MODEL), ln1_b=ln1_b.astype(F32).reshape(1, D_MODEL),
        wup_c=_chunk_ff(ffn_w_up, 1).astype(BF16),
        cw_c=_chunk_ff(ffn_conv_w.astype(F32), 1),
        cb_c=_chunk_ff(ffn_conv_b.astype(F32).reshape(1, -1), 1),
        wdn_c=ffn_w_down.astype(BF16).reshape(N_FF_CHUNKS, FF_CHUNK, D_MODEL),
        ln2_g=ln2_g.astype(F32).reshape(1, D_MODEL), ln2_b=ln2_b.astype(F32).reshape(1, D_MODEL),
    )


def _layer(x2d, p, batch, seq, alpha):
    na_qkv, g_qkvz, gates = _in_proj(x2d, p["w_pad"], p["conv_w"], p["alog_lane"], p["dtb_lane"], seq)
    na_out = _na_attention(na_qkv, p["bias_tbl"], batch, seq)
    gdn_out = _gdn(g_qkvz, gates, p["norm_w"], batch, seq)
    x1 = _out_proj(na_out, gdn_out, x2d, p["w_out"], p["ln1_g"], p["ln1_b"], alpha)
    return _ffn(x1, p["wup_c"], p["cw_c"], p["cb_c"], p["wdn_c"], p["ln2_g"], p["ln2_b"], batch, seq, alpha)


def _trunk(x, layer_params, alpha):
    batch, seq, d = x.shape
    x2d = x.reshape(batch * seq, d)
    for p in layer_params:
        x2d = _layer(x2d, p, batch, seq, alpha)
    return x2d.reshape(batch, seq, d)


def kernel(x_prompt, x_sample, w_in, na_rpb, gdn_conv_w, gdn_a_log, gdn_dt_bias, gdn_norm_w, w_out, ln1_g, ln1_b,
           ffn_w_up, ffn_conv_w, ffn_conv_b, ffn_w_down, ln2_g, ln2_b):
    depth = w_in.shape[0]
    alpha = float((2 * depth) ** 0.25)
    stacked = (w_in, na_rpb, gdn_conv_w, gdn_a_log, gdn_dt_bias, gdn_norm_w, w_out, ln1_g, ln1_b,
               ffn_w_up, ffn_conv_w, ffn_conv_b, ffn_w_down, ln2_g, ln2_b)
    layer_params = [_prep_layer_params(*(a[l] for a in stacked)) for l in range(depth)]
    return (_trunk(x_prompt, layer_params, alpha), _trunk(x_sample, layer_params, alpha))
```
